```python
import math
import jax
import jax.numpy as jnp
from jax import lax
import numpy as np

D_MODEL = 2048
BATCH = 16
SEQ = 256
DEPTH = 4
DEC_BATCH = 8
DEC_SEQ = 1024
PAST_LEN = 512

GRID_W = 64
N_MIXERS = 3
N_ATTN = (DEPTH + 2) // 3
N_HYENA = (DEPTH + 1) // 3
N_RWKV = DEPTH // 3

ATTN_HEADS = 16
ATTN_KV_HEADS = 4
ATTN_HEAD_DIM = 128
ATTN_GROUP = ATTN_HEADS // ATTN_KV_HEADS
WINDOW = 128
ATTN_BLOCK = 128
ROPE_BASE = 10000.0
NEG_INF = -1e30

HYENA_EMB = 33
HYENA_FILTER_WIDTH = 64
HYENA_TARGET = 1e-2
HYENA_FAST_DECAY = 0.3
HYENA_SLOW_DECAY = 1.5

RWKV_HEAD_DIM = 64
RWKV_HEADS = D_MODEL // RWKV_HEAD_DIM
RWKV_DECAY_LORA = 96
RWKV_A_LORA = 96
RWKV_GATE_LORA = 256
RWKV_GN_EPS = 64e-5

D_FF = 4 * D_MODEL
NORM_EPS = 1e-6

kernel_name = "hybrid_diffusion_attn_hyena_rwkv7_step"


def _rmsnorm(x, g):
    xf = x.astype(jnp.float32)
    y = xf * lax.rsqrt(jnp.mean(xf * xf, axis=-1, keepdims=True) + NORM_EPS)
    return (y * g.astype(jnp.float32)).astype(x.dtype)


def _ada(cond, w, b):
    m = jax.nn.silu(cond) @ w + b
    return jnp.split(m[:, None, :], 6, axis=-1)


def _modulate(h, shift, scale):
    return h * (1 + scale) + shift


def _mlp(h, w1, w2):
    return jnp.square(jax.nn.relu(h @ w1)) @ w2


def _axial_angles(L):
    rows = L // GRID_W
    t = jnp.arange(rows * GRID_W)
    row = (t // GRID_W).astype(jnp.float32)
    col = (t % GRID_W).astype(jnp.float32)
    half = ATTN_HEAD_DIM // 2
    inv = ROPE_BASE ** (-jnp.arange(0, half, 2, dtype=jnp.float32) / half)
    return row[:, None] * inv, col[:, None] * inv


def _rotate(x, ang):
    x1, x2 = jnp.split(x, 2, axis=-1)
    c, s = jnp.cos(ang), jnp.sin(ang)
    return jnp.concatenate([x1 * c - x2 * s, x1 * s + x2 * c], axis=-1)


def _apply_axial_rope(x, ang_row, ang_col):
    L = x.shape[1]
    shp = (L,) + (1,) * (x.ndim - 3) + (ang_row.shape[-1],)
    xr, xc = jnp.split(x.astype(jnp.float32), 2, axis=-1)
    out = jnp.concatenate([_rotate(xr, ang_row.reshape(shp)), _rotate(xc, ang_col.reshape(shp))], axis=-1)
    return out.astype(x.dtype)


def _attn_qkv(h, wqkv):
    B, L, _ = h.shape
    qkv = h @ wqkv
    qd = ATTN_HEADS * ATTN_HEAD_DIM
    kd = ATTN_KV_HEADS * ATTN_HEAD_DIM
    q = qkv[..., :qd].reshape(B, L, ATTN_KV_HEADS, ATTN_GROUP, ATTN_HEAD_DIM)
    k = qkv[..., qd:qd + kd].reshape(B, L, ATTN_KV_HEADS, ATTN_HEAD_DIM)
    v = qkv[..., qd + kd:].reshape(B, L, ATTN_KV_HEADS, ATTN_HEAD_DIM)
    return q, k, v


def _softmax_with_sink(s, sink):
    sk = jnp.broadcast_to(sink.astype(jnp.float32).reshape(ATTN_KV_HEADS, ATTN_GROUP, 1, 1), s.shape[:-1] + (1,))
    p = jax.nn.softmax(jnp.concatenate([s, sk], axis=-1), axis=-1)
    return p[..., :-1]


def _attn_context(h, wqkv, wo, sink):
    B, L, _ = h.shape
    q, k, v = _attn_qkv(h, wqkv)
    nq = L // ATTN_BLOCK
    qb = jnp.moveaxis(q.reshape(B, nq, ATTN_BLOCK, ATTN_KV_HEADS, ATTN_GROUP, ATTN_HEAD_DIM), 1, 0)
    scale = ATTN_HEAD_DIM ** -0.5

    def block(qblk):
        s = jnp.einsum("bqhgd,bshd->bhgqs", qblk, k).astype(jnp.float32) * scale
        p = _softmax_with_sink(s, sink)
        return jnp.einsum("bhgqs,bshd->bqhgd", p.astype(v.dtype), v)

    o = lax.map(block, qb)
    o = jnp.moveaxis(o, 0, 1).reshape(B, L, ATTN_HEADS * ATTN_HEAD_DIM)
    return o @ wo, k, v


def _attn_latent(h, k_ctx, v_ctx, wqkv, wo, sink):
    B, L, _ = h.shape
    q, k, v = _attn_qkv(h, wqkv)
    ang_r, ang_c = _axial_angles(L)
    q = _apply_axial_rope(q, ang_r, ang_c)
    k = _apply_axial_rope(k, ang_r, ang_c)
    nb = L // ATTN_BLOCK
    nl = 3 * ATTN_BLOCK

    def band(t):
        tp = jnp.pad(t, ((0, 0), (ATTN_BLOCK, ATTN_BLOCK), (0, 0), (0, 0)))
        tp = tp.reshape(B, nb + 2, ATTN_BLOCK, ATTN_KV_HEADS, ATTN_HEAD_DIM)
        tb = jnp.concatenate([tp[:, :-2], tp[:, 1:-1], tp[:, 2:]], axis=2)
        return jnp.moveaxis(tb, 1, 0)

    k_band, v_band = band(k), band(v)
    qb = jnp.moveaxis(q.reshape(B, nb, ATTN_BLOCK, ATTN_KV_HEADS, ATTN_GROUP, ATTN_HEAD_DIM), 1, 0)
    qi = jnp.arange(ATTN_BLOCK)
    ki = jnp.arange(nl)
    rel = ki[None, :] - ATTN_BLOCK - qi[:, None]
    scale = ATTN_HEAD_DIM ** -0.5

    def block(args):
        j, qblk, kblk, vblk = args
        kpos = (j - 1) * ATTN_BLOCK + ki
        valid = (jnp.abs(rel) <= WINDOW) & ((kpos >= 0) & (kpos < L))[None, :]
        s_loc = jnp.einsum("bqhgd,bshd->bhgqs", qblk, kblk).astype(jnp.float32) * scale
        s_loc = jnp.where(valid, s_loc, NEG_INF)
        s_ctx = jnp.einsum("bqhgd,bshd->bhgqs", qblk, k_ctx).astype(jnp.float32) * scale
        p = _softmax_with_sink(jnp.concatenate([s_loc, s_ctx], axis=-1), sink).astype(v.dtype)
        return (jnp.einsum("bhgqs,bshd->bqhgd", p[..., :nl], vblk)
                + jnp.einsum("bhgqs,bshd->bqhgd", p[..., nl:], v_ctx))

    o = lax.map(block, (jnp.arange(nb), qb, k_band, v_band))
    o = jnp.moveaxis(o, 0, 1).reshape(B, L, ATTN_HEADS * ATTN_HEAD_DIM)
    return o @ wo


def _centred_conv3(u, w, b):
    up = jnp.pad(u, ((0, 0), (1, 1), (0, 0)))
    return up[:, :-2] * w[0] + up[:, 1:-1] * w[1] + up[:, 2:] * w[2] + b


def _hyena_filter(L, f_w1, f_b1, f_w2, f_b2, f_w3, f_b3, freq, f_wout):
    f32 = jnp.float32
    t = jnp.linspace(0.0, 1.0, L, dtype=f32)[:, None]
    bands = (HYENA_EMB - 1) // 2
    w = 2 * math.pi * jnp.arange(L, dtype=f32)[:, None] / L
    fr = jnp.linspace(1e-4, bands - 1, bands, dtype=f32)[None, :]
    z = jnp.concatenate([t, jnp.cos(fr * w), -jnp.sin(fr * w)], axis=-1)
    fq = freq.astype(f32)
    hdn = jnp.sin(fq * (z @ f_w1.astype(f32) + f_b1.astype(f32)))
    hdn = jnp.sin(fq * (hdn @ f_w2.astype(f32) + f_b2.astype(f32)))
    hdn = jnp.sin(fq * (hdn @ f_w3.astype(f32) + f_b3.astype(f32)))
    filt = (hdn @ f_wout.astype(f32)).reshape(L, 2, D_MODEL)
    deltas = jnp.linspace(math.log(HYENA_TARGET) / HYENA_SLOW_DECAY,
                          math.log(HYENA_TARGET) / HYENA_FAST_DECAY, D_MODEL, dtype=f32)
    filt = filt * jnp.exp(-t * jnp.abs(deltas))[:, None, :]
    h_fwd, h_bwd = filt[:, 0], filt[:, 1]
    return jnp.concatenate([h_fwd, jnp.zeros((1, D_MODEL), f32), h_bwd[:0:-1]], axis=0)


def _hyena(h, w_in, conv_w, conv_b, f_w1, f_b1, f_w2, f_b2, f_w3, f_b3, freq, f_wout, fbias, w_out):
    B, L, _ = h.shape
    u = _centred_conv3(h @ w_in, conv_w, conv_b)
    x0, x1, v = jnp.split(u, 3, axis=-1)
    z = (v * x1).astype(jnp.float32)
    h_circ = _hyena_filter(L, f_w1, f_b1, f_w2, f_b2, f_w3, f_b3, freq, f_wout)
    zf = jnp.fft.rfft(z, n=2 * L, axis=1)
    hf = jnp.fft.rfft(h_circ, axis=0)
    y = jnp.fft.irfft(zf * hf[None], n=2 * L, axis=1)[:, :L] + z * fbias.astype(jnp.float32)
    return (y.astype(h.dtype) * x0) @ w_out


def _rwkv_step(S, inp):
    r, w, kk, b, k, v = inp
    sa = jnp.einsum("bhvk,bhk->bhv", S, -kk)
    S = S * w[:, :, None, :] + sa[..., None] * b[:, :, None, :] + v[..., None] * k[:, :, None, :]
    return S, jnp.einsum("bhvk,bhk->bhv", S, r)


def _rwkv_scan(S0, r, w, kk, b, k, v, reverse):
    xs = tuple(jnp.moveaxis(t, 1, 0) for t in (r, w, kk, b, k, v))
    S, ys = lax.scan(_rwkv_step, S0.astype(jnp.float32), xs, reverse=reverse)
    return S, jnp.moveaxis(ys, 0, 1)


def _rwkv(h, s0_fwd, s0_bwd, mu, wr, wk, wv, wo, w0, w1, w2, a0, a1, a2, g1, g2, k_k, k_a, r_k, ln_w, ln_b):
    f32 = jnp.float32
    B, L, D = h.shape
    hp = jnp.pad(h, ((0, 0), (1, 1), (0, 0)))
    xx = 0.5 * (hp[:, :-2] + hp[:, 2:]) - h
    xr, xw, xk, xv, xa, xg = [h + xx * mu[i] for i in range(6)]

    def heads(t):
        return t.astype(f32).reshape(B, L, RWKV_HEADS, RWKV_HEAD_DIM)

    r = heads(xr @ wr)
    k = heads(xk @ wk)
    v = heads(xv @ wv)
    g = jax.nn.sigmoid(xg @ g1) @ g2
    kk = k * k_k.astype(f32).reshape(RWKV_HEADS, RWKV_HEAD_DIM)
    kk = kk * lax.rsqrt(jnp.sum(kk * kk, axis=-1, keepdims=True) + 1e-12)
    k_a_h = k_a.astype(f32).reshape(RWKV_HEADS, RWKV_HEAD_DIM)
    r_k_h = r_k.astype(f32)
    ys, bonuses, finals = [], [], []
    for d, (s0, rev) in enumerate(((s0_fwd, False), (s0_bwd, True))):
        wlog = -jax.nn.softplus(-(w0[d] + jnp.tanh(xw @ w1[d]) @ w2[d])) - 0.5
        decay = heads(jnp.exp(-jnp.exp(wlog.astype(f32))))
        a = heads(jax.nn.sigmoid(a0[d] + (xa @ a1[d]) @ a2[d]))
        kd = k * (1 + (a - 1) * k_a_h)
        s_fin, y_d = _rwkv_scan(s0, r, decay, kk, kk * a, kd, v, rev)
        ys.append(y_d)
        bonuses.append(jnp.sum(r * kd * r_k_h, axis=-1, keepdims=True) * v)
        finals.append(s_fin)
    y = ys[0] + ys[1]
    mean = jnp.mean(y, axis=-1, keepdims=True)
    var = jnp.mean(jnp.square(y - mean), axis=-1, keepdims=True)
    yn = ((y - mean) * lax.rsqrt(var + RWKV_GN_EPS)).reshape(B, L, D) * ln_w.astype(f32) + ln_b.astype(f32)
    out = (yn + (bonuses[0] + bonuses[1]).reshape(B, L, D)).astype(h.dtype) * g
    return out @ wo, jnp.stack(finals, axis=1).astype(h.dtype)


def setup_inputs(seed: int = 0) -> dict:
    key = jax.random.key(seed)
    keys = jax.random.split(key, 64)
    counter = iter(range(64))
    f32 = jnp.float32

    def nrm(shape, scale=1.0):
        return scale * jax.random.normal(keys[next(counter)], shape, f32)

    D = D_MODEL
    QD = ATTN_HEADS * ATTN_HEAD_DIM
    QKV = (ATTN_HEADS + 2 * ATTN_KV_HEADS) * ATTN_HEAD_DIM
    HF = HYENA_FILTER_WIDTH
    w0_base = jnp.linspace(-6.0, -0.5, D, dtype=f32)
    return {
        "x_prompt": nrm((BATCH, SEQ, D)),
        "x_sample": nrm((DEC_BATCH, DEC_SEQ, D)),
        "cache_attn_k": nrm((DEC_BATCH, N_ATTN, PAST_LEN, ATTN_KV_HEADS, ATTN_HEAD_DIM)),
        "cache_attn_v": nrm((DEC_BATCH, N_ATTN, PAST_LEN, ATTN_KV_HEADS, ATTN_HEAD_DIM)),
        "state_rwkv": nrm((DEC_BATCH, N_RWKV, 2, RWKV_HEADS, RWKV_HEAD_DIM, RWKV_HEAD_DIM), 0.3),
        "c": nrm((DEC_BATCH, D)),
        "c_ctx": nrm((D,)),
        "ada_w": nrm((DEPTH, D, 6 * D), 0.5 * D ** -0.5),
        "ada_b": nrm((DEPTH, 6 * D), 0.01),
        "norm1_g": 1.0 + nrm((DEPTH, D), 0.05),
        "norm2_g": 1.0 + nrm((DEPTH, D), 0.05),
        "mlp_w1": nrm((DEPTH, D, D_FF), D ** -0.5),
        "mlp_w2": nrm((DEPTH, D_FF, D), D_FF ** -0.5),
        "attn_wqkv": nrm((N_ATTN, D, QKV), D ** -0.5),
        "attn_wo": nrm((N_ATTN, QD, D), QD ** -0.5),
        "attn_sink": nrm((N_ATTN, ATTN_HEADS), 0.5),
        "hy_w_in": nrm((N_HYENA, D, 3 * D), D ** -0.5),
        "hy_conv_w": nrm((N_HYENA, 3, 3 * D), 3 ** -0.5),
        "hy_conv_b": nrm((N_HYENA, 3 * D), 0.01),
        "hy_f_w1": nrm((N_HYENA, HYENA_EMB, HF), 1.0),
        "hy_f_b1": nrm((N_HYENA, HF), 0.1),
        "hy_f_w2": nrm((N_HYENA, HF, HF), HF ** -0.5),
        "hy_f_b2": nrm((N_HYENA, HF), 0.1),
        "hy_f_w3": nrm((N_HYENA, HF, HF), HF ** -0.5),
        "hy_f_b3": nrm((N_HYENA, HF), 0.1),
        "hy_f_freq": 1.0 + nrm((N_HYENA, HF), 0.05),
        "hy_f_wout": nrm((N_HYENA, HF, 2 * D), 0.02 * HF ** -0.5),
        "hy_bias": nrm((N_HYENA, D), 0.1),
        "hy_w_out": nrm((N_HYENA, D, D), D ** -0.5),
        "rw_mu": 0.5 + nrm((N_RWKV, 6, D), 0.1),
        "rw_wr": nrm((N_RWKV, D, D), D ** -0.5),
        "rw_wk": nrm((N_RWKV, D, D), D ** -0.5),
        "rw_wv": nrm((N_RWKV, D, D), D ** -0.5),
        "rw_wo": nrm((N_RWKV, D, D), D ** -0.5),
        "rw_w0": w0_base + nrm((N_RWKV, 2, D), 0.1),
        "rw_w1": nrm((N_RWKV, 2, D, RWKV_DECAY_LORA), D ** -0.5),
        "rw_w2": nrm((N_RWKV, 2, RWKV_DECAY_LORA, D), 0.1 * RWKV_DECAY_LORA ** -0.5),
        "rw_a0": nrm((N_RWKV, 2, D), 0.1),
        "rw_a1": nrm((N_RWKV, 2, D, RWKV_A_LORA), D ** -0.5),
        "rw_a2": nrm((N_RWKV, 2, RWKV_A_LORA, D), 0.5 * RWKV_A_LORA ** -0.5),
        "rw_g1": nrm((N_RWKV, D, RWKV_GATE_LORA), D ** -0.5),
        "rw_g2": nrm((N_RWKV, RWKV_GATE_LORA, D), RWKV_GATE_LORA ** -0.5),
        "rw_k_k": 0.85 + nrm((N_RWKV, D), 0.02),
        "rw_k_a": 1.0 + nrm((N_RWKV, D), 0.02),
        "rw_r_k": nrm((N_RWKV, RWKV_HEADS, RWKV_HEAD_DIM), 0.1),
        "rw_ln_w": 1.0 + nrm((N_RWKV, D), 0.05),
        "rw_ln_b": nrm((N_RWKV, D), 0.01),
        "final_norm_g": 1.0 + nrm((D,), 0.05),
    }


def reference(x_prompt, x_sample, cache_attn_k, cache_attn_v, state_rwkv, c, c_ctx,
              ada_w, ada_b, norm1_g, norm2_g, mlp_w1, mlp_w2,
              attn_wqkv, attn_wo, attn_sink,
              hy_w_in, hy_conv_w, hy_conv_b, hy_f_w1, hy_f_b1, hy_f_w2, hy_f_b2, hy_f_w3, hy_f_b3,
              hy_f_freq, hy_f_wout, hy_bias, hy_w_out,
              rw_mu, rw_wr, rw_wk, rw_wv, rw_wo, rw_w0, rw_w1, rw_w2, rw_a0, rw_a1, rw_a2,
              rw_g1, rw_g2, rw_k_k, rw_k_a, rw_r_k, rw_ln_w, rw_ln_b,
              final_norm_g):
    xp = x_prompt
    xs = x_sample
    new_k, new_v, new_s = [], [], []
    cond_ctx = c_ctx[None, :]
    for i in range(DEPTH):
        kind = i % N_MIXERS
        j = i // N_MIXERS
        sh1p, sc1p, gt1p, sh2p, sc2p, gt2p = _ada(cond_ctx, ada_w[i], ada_b[i])
        sh1s, sc1s, gt1s, sh2s, sc2s, gt2s = _ada(c, ada_w[i], ada_b[i])
        hp = _modulate(_rmsnorm(xp, norm1_g[i]), sh1p, sc1p)
        hs = _modulate(_rmsnorm(xs, norm1_g[i]), sh1s, sc1s)
        if kind == 0:
            op, kc, vc = _attn_context(hp, attn_wqkv[j], attn_wo[j], attn_sink[j])
            os_ = _attn_latent(hs, cache_attn_k[:, j], cache_attn_v[:, j], attn_wqkv[j], attn_wo[j], attn_sink[j])
            new_k.append(kc)
            new_v.append(vc)
        elif kind == 1:
            hy = (hy_w_in[j], hy_conv_w[j], hy_conv_b[j], hy_f_w1[j], hy_f_b1[j], hy_f_w2[j], hy_f_b2[j],
                  hy_f_w3[j], hy_f_b3[j], hy_f_freq[j], hy_f_wout[j], hy_bias[j], hy_w_out[j])
            op = _hyena(hp, *hy)
            os_ = _hyena(hs, *hy)
        else:
            rw = (rw_mu[j], rw_wr[j], rw_wk[j], rw_wv[j], rw_wo[j], rw_w0[j], rw_w1[j], rw_w2[j],
                  rw_a0[j], rw_a1[j], rw_a2[j], rw_g1[j], rw_g2[j], rw_k_k[j], rw_k_a[j], rw_r_k[j],
                  rw_ln_w[j], rw_ln_b[j])
            zeros = jnp.zeros((xp.shape[0], RWKV_HEADS, RWKV_HEAD_DIM, RWKV_HEAD_DIM), jnp.float32)
            op, st = _rwkv(hp, zeros, zeros, *rw)
            os_, _ = _rwkv(hs, state_rwkv[:, j, 0], state_rwkv[:, j, 1], *rw)
            new_s.append(st)
        xp = xp + gt1p * op
        xs = xs + gt1s * os_
        hp = _modulate(_rmsnorm(xp, norm2_g[i]), sh2p, sc2p)
        hs = _modulate(_rmsnorm(xs, norm2_g[i]), sh2s, sc2s)
        xp = xp + gt2p * _mlp(hp, mlp_w1[i], mlp_w2[i])
        xs = xs + gt2s * _mlp(hs, mlp_w1[i], mlp_w2[i])
    y_prompt = _rmsnorm(xp, final_norm_g)
    y_sample = _rmsnorm(xs, final_norm_g)
    new_attn_k = jnp.stack(new_k, axis=1)
    new_attn_v = jnp.stack(new_v, axis=1)
    new_rwkv_state = jnp.stack(new_s, axis=1)
    return (y_prompt, y_sample, new_attn_k, new_attn_v, new_rwkv_state)
```

```python
import functools
import math

import numpy as np
import jax
import jax.numpy as jnp
from jax import lax
from jax.experimental import pallas as pl
from jax.experimental.pallas import tpu as pltpu

F32 = jnp.float32
BF16 = jnp.bfloat16

D_MODEL = 2048
BATCH = 16
SEQ = 256
DEPTH = 4
DEC_BATCH = 8
DEC_SEQ = 1024
PAST_LEN = 512
GRID_W = 64
N_MIXERS = 3

ATTN_HEADS = 16
ATTN_KV_HEADS = 4
ATTN_HEAD_DIM = 128
ATTN_GROUP = ATTN_HEADS // ATTN_KV_HEADS
WINDOW = 128
ATTN_BLOCK = 128
ROPE_BASE = 10000.0
NEG_INF = -1e30

HYENA_EMB = 33
HYENA_TARGET = 1e-2
HYENA_FAST_DECAY = 0.3
HYENA_SLOW_DECAY = 1.5

RWKV_HEAD_DIM = 64
RWKV_HEADS = D_MODEL // RWKV_HEAD_DIM
RWKV_GN_EPS = 64e-5
RWKV_CHUNK = 64
NORM_EPS = 1e-6
D_FF = 4 * D_MODEL

T_P = BATCH * SEQ
T_S = DEC_BATCH * DEC_SEQ
T_ALL = T_P + T_S
N_COND = 16

LANE = 128
V7X_VMEM_LIMIT = 56 * 1024 * 1024
HIGHEST = lax.Precision.HIGHEST


def _params(n_axes, vmem_bytes):
    return pltpu.CompilerParams(dimension_semantics=("arbitrary",) * n_axes,
                                vmem_limit_bytes=int(min(max(vmem_bytes, 16 * 2**20), V7X_VMEM_LIMIT)))


def _cond_row(i, tm):
    n_p = T_P // tm
    per = DEC_SEQ // tm
    return jnp.where(i < n_p, 0, 1 + (i - n_p) // per)


def _dot_nt(a, b, **kw):
    return lax.dot_general(a, b, (((1,), (1,)), ((), ())), preferred_element_type=F32, **kw)


def _dot_tn(a, b, **kw):
    return lax.dot_general(a, b, (((0,), (0,)), ((), ())), preferred_element_type=F32, **kw)


def _mm_body(*refs, nk, cast_w, act, has_bias, has_res):
    it = iter(refs)
    a_ref = next(it)
    w_ref = next(it)
    bias_ref = next(it) if has_bias else None
    res_ref = next(it) if has_res else None
    gate_ref = next(it) if has_res else None
    o_ref = next(it)
    acc_ref = next(it) if nk > 1 else None
    wb_ref = next(it) if cast_w else None

    if cast_w:
        @pl.when(pl.program_id(1) == 0)
        def _():
            wb_ref[...] = w_ref[...].astype(BF16)
        w = wb_ref[...]
    else:
        w = w_ref[...].astype(BF16)
    p = jnp.dot(a_ref[...].astype(BF16), w, preferred_element_type=F32)

    def finish(acc):
        if has_bias:
            acc = acc + bias_ref[...]
        if act == "relu2":
            acc = jnp.square(jnp.maximum(acc, 0.0))
        elif act == "tanh":
            acc = jnp.tanh(acc)
        elif act == "sigmoid":
            acc = jax.nn.sigmoid(acc)
        if has_res:
            acc = res_ref[...] + gate_ref[...] * acc
        o_ref[...] = acc.astype(o_ref.dtype)

    if nk == 1:
        finish(p)
    else:
        k = pl.program_id(2)

        @pl.when(k == 0)
        def _():
            acc_ref[...] = p

        @pl.when(k > 0)
        def _():
            acc_ref[...] += p

        @pl.when(k == nk - 1)
        def _():
            finish(acc_ref[...])


def _matmul(a, w, *, tm, tn, tk=None, out_dtype=F32, act=None, bias=None, res=None, gate=None, name="matmul"):
    M, K = a.shape
    N = w.shape[1]
    tk = K if tk is None else tk
    nk = K // tk
    assert M % tm == 0 and N % tn == 0 and K % tk == 0
    cast_w = nk == 1 and w.dtype != BF16
    has_bias = bias is not None
    has_res = res is not None
    in_specs = [pl.BlockSpec((tm, tk), lambda j, i, k: (i, k)),
                pl.BlockSpec((tk, tn), lambda j, i, k: (k, j))]
    args = [a, w]
    if has_bias:
        in_specs.append(pl.BlockSpec((1, tn), lambda j, i, k: (0, j)))
        args.append(bias.reshape(1, N))
    if has_res:
        in_specs.append(pl.BlockSpec((tm, tn), lambda j, i, k: (i, j)))
        in_specs.append(pl.BlockSpec((None, 1, tn), lambda j, i, k: (_cond_row(i, tm), 0, j)))
        args += [res, gate]
    scratch = []
    if nk > 1:
        scratch.append(pltpu.VMEM((tm, tn), F32))
    if cast_w:
        scratch.append(pltpu.VMEM((tk, tn), BF16))
    vmem = (2 * (tm * tk * a.dtype.itemsize + tk * tn * w.dtype.itemsize + tm * tn * jnp.dtype(out_dtype).itemsize
                 + (tm * tn * 4 if has_res else 0)) + 3 * tm * tn * 4 + tk * tn * 2 + tm * tk * 2 + (4 << 20))
    return pl.pallas_call(
        functools.partial(_mm_body, nk=nk, cast_w=cast_w, act=act, has_bias=has_bias, has_res=has_res),
        grid=(N // tn, M // tm, nk),
        in_specs=in_specs,
        out_specs=pl.BlockSpec((tm, tn), lambda j, i, k: (i, j)),
        out_shape=jax.ShapeDtypeStruct((M, N), out_dtype),
        scratch_shapes=scratch,
        compiler_params=_params(3, vmem),
        name=name,
    )(*args)


def _ada_body(c_ref, w_ref, b_ref, o_ref):
    c = c_ref[...]
    s = (c * jax.nn.sigmoid(c)).astype(BF16)
    o_ref[...] = jnp.dot(s, w_ref[...].astype(BF16), preferred_element_type=F32) + b_ref[...]


def _ada_table(cond, ada_w, ada_b):
    tn = 1024
    n6 = 6 * D_MODEL
    return pl.pallas_call(
        _ada_body,
        grid=(DEPTH, n6 // tn),
        in_specs=[pl.BlockSpec((N_COND, D_MODEL), lambda l, j: (0, 0)),
                  pl.BlockSpec((None, D_MODEL, tn), lambda l, j: (l, 0, j)),
                  pl.BlockSpec((None, 1, tn), lambda l, j: (l, 0, j))],
        out_specs=pl.BlockSpec((None, N_COND, tn), lambda l, j: (l, 0, j)),
        out_shape=jax.ShapeDtypeStruct((DEPTH, N_COND, n6), F32),
        compiler_params=_params(2, 2 * D_MODEL * tn * 4 + D_MODEL * tn * 2 + (4 << 20)),
        name="ada_table",
    )(cond, ada_w, ada_b.reshape(DEPTH, 1, n6))


def _norm_body(x_ref, g_ref, sh_ref, sc_ref, o_ref, *, modulate):
    x = x_ref[...]
    y = x * lax.rsqrt(jnp.mean(x * x, axis=-1, keepdims=True) + NORM_EPS) * g_ref[...]
    if modulate:
        y = y * (1.0 + sc_ref[...]) + sh_ref[...]
    o_ref[...] = y.astype(o_ref.dtype)


def _norm_mod(x, g, shift, scale, out_dtype, modulate=True):
    tm = 512
    cond_spec = pl.BlockSpec((None, 1, D_MODEL), lambda i: (_cond_row(i, tm), 0, 0))
    return pl.pallas_call(
        functools.partial(_norm_body, modulate=modulate),
        grid=(T_ALL // tm,),
        in_specs=[pl.BlockSpec((tm, D_MODEL), lambda i: (i, 0)),
                  pl.BlockSpec((1, D_MODEL), lambda i: (0, 0)),
                  cond_spec, cond_spec],
        out_specs=pl.BlockSpec((tm, D_MODEL), lambda i: (i, 0)),
        out_shape=jax.ShapeDtypeStruct((T_ALL, D_MODEL), out_dtype),
        compiler_params=_params(1, 6 * tm * D_MODEL * 4 + (4 << 20)),
        name="norm_mod",
    )(x, g.reshape(1, D_MODEL), shift, scale)


def _softmax_sink_pv(parts, sink_col):
    m = sink_col
    for s, _ in parts:
        m = jnp.maximum(m, jnp.max(s, axis=-1, keepdims=True))
    denom = jnp.exp(sink_col - m)
    out = None
    for s, v in parts:
        p = jnp.exp(s - m)
        denom = denom + jnp.sum(p, axis=-1, keepdims=True)
        pv = jnp.dot(p.astype(BF16), v, preferred_element_type=F32)
        out = pv if out is None else out + pv
    return out / denom


def _attn_ctx_body(sink_ref, q_ref, k_ref, v_ref, o_ref):
    h = pl.program_id(1)
    scale = ATTN_HEAD_DIM ** -0.5
    k = k_ref[...].astype(BF16)
    v = v_ref[...].astype(BF16)
    for g in range(ATTN_GROUP):
        q = q_ref[:, g * ATTN_HEAD_DIM:(g + 1) * ATTN_HEAD_DIM].astype(BF16)
        s = _dot_nt(q, k) * scale
        sink = jnp.full((SEQ, 1), sink_ref[h * ATTN_GROUP + g], F32)
        o = _softmax_sink_pv([(s, v)], sink)
        o_ref[:, g * ATTN_HEAD_DIM:(g + 1) * ATTN_HEAD_DIM] = o.astype(o_ref.dtype)


def _attn_context(qkv, sink):
    gw = ATTN_GROUP * ATTN_HEAD_DIM
    kcol = ATTN_HEADS * ATTN_HEAD_DIM // ATTN_HEAD_DIM
    vcol = kcol + ATTN_KV_HEADS
    return pl.pallas_call(
        _attn_ctx_body,
        grid=(BATCH, ATTN_KV_HEADS),
        in_specs=[pl.BlockSpec(memory_space=pltpu.SMEM),
                  pl.BlockSpec((SEQ, gw), lambda b, h: (b, h)),
                  pl.BlockSpec((SEQ, ATTN_HEAD_DIM), lambda b, h: (b, kcol + h)),
                  pl.BlockSpec((SEQ, ATTN_HEAD_DIM), lambda b, h: (b, vcol + h))],
        out_specs=pl.BlockSpec((SEQ, gw), lambda b, h: (b, h)),
        out_shape=jax.ShapeDtypeStruct((T_P, ATTN_HEADS * ATTN_HEAD_DIM), BF16),
        compiler_params=_params(2, 16 << 20),
        name="attn_context",
    )(sink, qkv, qkv, qkv)


def _rope(x, cos, sin_lo, sin_hi):
    return (x * cos + pltpu.roll(x, ATTN_HEAD_DIM - 32, axis=1) * sin_lo
            + pltpu.roll(x, 32, axis=1) * sin_hi)


def _attn_lat_body(sink_ref, q_ref, k_ref, v_ref, ck_ref, cv_ref, cos_ref, slo_ref, shi_ref, o_ref, kr_ref):
    h = pl.program_id(1)
    scale = ATTN_HEAD_DIM ** -0.5
    blk = ATTN_BLOCK
    nb = DEC_SEQ // blk
    rows = ATTN_GROUP * blk
    kr_ref[...] = _rope(k_ref[...], cos_ref[...], slo_ref[...], shi_ref[...]).astype(BF16)
    ck = ck_ref[...].astype(BF16)
    cv = cv_ref[...].astype(BF16)
    row_head = lax.broadcasted_iota(jnp.int32, (rows, 1), 0) // blk
    sink = jnp.zeros((rows, 1), F32)
    for g in range(ATTN_GROUP):
        sink = jnp.where(row_head == g, sink_ref[h * ATTN_GROUP + g], sink)
    q_in_blk = lax.broadcasted_iota(jnp.int32, (rows, 3 * blk), 0) % blk
    k_in_band = lax.broadcasted_iota(jnp.int32, (rows, 3 * blk), 1)

    def block(j, carry):
        r0 = pl.multiple_of(j * blk, blk)
        k0 = pl.multiple_of(jnp.clip(j - 1, 0, nb - 3) * blk, blk)
        cos = cos_ref[pl.ds(r0, blk), :]
        slo = slo_ref[pl.ds(r0, blk), :]
        shi = shi_ref[pl.ds(r0, blk), :]
        qs = jnp.concatenate(
            [_rope(q_ref[pl.ds(r0, blk), g * ATTN_HEAD_DIM:(g + 1) * ATTN_HEAD_DIM], cos, slo, shi)
             for g in range(ATTN_GROUP)], axis=0).astype(BF16)
        kl = kr_ref[pl.ds(k0, 3 * blk), :]
        vl = v_ref[pl.ds(k0, 3 * blk), :].astype(BF16)
        rel = (k0 + k_in_band) - (r0 + q_in_blk)
        s_loc = jnp.where(jnp.abs(rel) <= WINDOW, _dot_nt(qs, kl) * scale, NEG_INF)
        s_ctx = _dot_nt(qs, ck) * scale
        o = _softmax_sink_pv([(s_loc, vl), (s_ctx, cv)], sink)
        for g in range(ATTN_GROUP):
            o_ref[pl.ds(r0, blk), g * ATTN_HEAD_DIM:(g + 1) * ATTN_HEAD_DIM] = (
                o[g * blk:(g + 1) * blk].astype(o_ref.dtype))
        return carry

    lax.fori_loop(0, nb, block, 0)


def _rope_tables():
    t = np.arange(DEC_SEQ)
    half = ATTN_HEAD_DIM // 2
    inv = ROPE_BASE ** (-np.arange(0, half, 2, dtype=np.float64) / half)
    ang_r = (t // GRID_W).astype(np.float64)[:, None] * inv
    ang_c = (t % GRID_W).astype(np.float64)[:, None] * inv
    ang = np.concatenate([ang_r, ang_r, ang_c, ang_c], axis=1)
    lane = np.arange(ATTN_HEAD_DIM)
    first = (lane % 64) < 32
    cos = np.cos(ang)
    sin = np.sin(ang)
    sin_lo = np.where(first, -sin, 0.0)
    sin_hi = np.where(first, 0.0, sin)
    return (jnp.asarray(cos, F32), jnp.asarray(sin_lo, F32), jnp.asarray(sin_hi, F32))


def _attn_latent(qkv, cache_k, cache_v, layer, sink):
    gw = ATTN_GROUP * ATTN_HEAD_DIM
    kcol = ATTN_HEADS
    vcol = kcol + ATTN_KV_HEADS
    rb0 = T_P // DEC_SEQ
    cos, slo, shi = _rope_tables()
    tab = pl.BlockSpec((DEC_SEQ, ATTN_HEAD_DIM), lambda b, h: (0, 0))
    cache_spec = pl.BlockSpec((None, None, PAST_LEN, ATTN_HEAD_DIM), lambda b, h: (b, layer, 0, h))
    return pl.pallas_call(
        _attn_lat_body,
        grid=(DEC_BATCH, ATTN_KV_HEADS),
        in_specs=[pl.BlockSpec(memory_space=pltpu.SMEM),
                  pl.BlockSpec((DEC_SEQ, gw), lambda b, h: (rb0 + b, h)),
                  pl.BlockSpec((DEC_SEQ, ATTN_HEAD_DIM), lambda b, h: (rb0 + b, kcol + h)),
                  pl.BlockSpec((DEC_SEQ, ATTN_HEAD_DIM), lambda b, h: (rb0 + b, vcol + h)),
                  cache_spec, cache_spec, tab, tab, tab],
        out_specs=pl.BlockSpec((DEC_SEQ, gw), lambda b, h: (b, h)),
        out_shape=jax.ShapeDtypeStruct((T_S, ATTN_HEADS * ATTN_HEAD_DIM), BF16),
        scratch_shapes=[pltpu.VMEM((DEC_SEQ, ATTN_HEAD_DIM), BF16)],
        compiler_params=_params(2, 32 << 20),
        name="attn_latent",
    )(sink, qkv, qkv, qkv, cache_k, cache_v, cos, slo, shi)


def _dft_matrices(L):
    N = 2 * L
    p = np.arange(N)[:, None]
    n = np.arange(N)[None, :]
    f = np.where(p < L, p, p - L)
    ang = 2.0 * np.pi * ((f * n) % N) / N
    fwd = np.where(p < L, np.cos(ang), np.where(p == L, np.cos(np.pi * n), -np.sin(ang)))
    t = np.arange(L)[:, None]
    pp = np.arange(N)[None, :]
    ff = np.where(pp < L, pp, pp - L)
    ang2 = 2.0 * np.pi * ((ff * t) % N) / N
    inv = np.where(pp < L, np.where(pp == 0, 1.0, 2.0) * np.cos(ang2),
                   np.where(pp == L, np.cos(np.pi * t), -2.0 * np.sin(ang2))) / N
    return fwd, inv


def _hyena_filter_time(L, f_w1, f_b1, f_w2, f_b2, f_w3, f_b3, freq, f_wout):
    t = jnp.linspace(0.0, 1.0, L, dtype=F32)[:, None]
    bands = (HYENA_EMB - 1) // 2
    w = 2 * math.pi * jnp.arange(L, dtype=F32)[:, None] / L
    fr = jnp.linspace(1e-4, bands - 1, bands, dtype=F32)[None, :]
    z = jnp.concatenate([t, jnp.cos(fr * w), -jnp.sin(fr * w)], axis=-1)
    hdn = jnp.sin(freq * (z @ f_w1 + f_b1))
    hdn = jnp.sin(freq * (hdn @ f_w2 + f_b2))
    hdn = jnp.sin(freq * (hdn @ f_w3 + f_b3))
    filt = (hdn @ f_wout).reshape(L, 2, D_MODEL)
    deltas = jnp.linspace(math.log(HYENA_TARGET) / HYENA_SLOW_DECAY,
                          math.log(HYENA_TARGET) / HYENA_FAST_DECAY, D_MODEL, dtype=F32)
    filt = filt * jnp.exp(-t * jnp.abs(deltas))[:, None, :]
    return jnp.concatenate([filt[:, 0], jnp.zeros((1, D_MODEL), F32), filt[:0:-1, 1]], axis=0)


def _hyena_body(x0_ref, x1_ref, v_ref, cw0_ref, cw1_ref, cwv_ref, cb0_ref, cb1_ref, cbv_ref,
                hf_ref, fb_ref, fwd_ref, inv_ref, o_ref, *, L):
    row = lax.broadcasted_iota(jnp.int32, (L, 1), 0)

    def conv3(u_ref, w_ref, b_ref):
        u = u_ref[...]
        prev = jnp.where(row == 0, 0.0, pltpu.roll(u, 1, axis=0))
        nxt = jnp.where(row == L - 1, 0.0, pltpu.roll(u, L - 1, axis=0))
        return prev * w_ref[0:1, :] + u * w_ref[1:2, :] + nxt * w_ref[2:3, :] + b_ref[...]

    x0 = conv3(x0_ref, cw0_ref, cb0_ref)
    x1 = conv3(x1_ref, cw1_ref, cb1_ref)
    v = conv3(v_ref, cwv_ref, cbv_ref)
    z = v * x1
    zf = jnp.dot(fwd_ref[...], z.astype(BF16), preferred_element_type=F32)
    zt, zb = zf[:L], zf[L:]
    ht, hb = hf_ref[0:L, :], hf_ref[L:2 * L, :]
    first = row == 0
    yt = zt * ht - jnp.where(first, 0.0, zb * hb)
    yb = jnp.where(first, zb * hb, zt * hb + zb * ht)
    yf = jnp.concatenate([yt, yb], axis=0).astype(BF16)
    y = jnp.dot(inv_ref[...], yf, preferred_element_type=F32) + z * fb_ref[...]
    o_ref[...] = (y * x0).astype(o_ref.dtype)


def _hyena_mix(u, row0, nseq, L, conv_w, conv_b, hf, fbias):
    tc = 256
    nc = D_MODEL // tc
    fwd, inv = _dft_matrices(L)
    fwd = jnp.asarray(fwd[:, :L], F32).astype(BF16)
    inv = jnp.asarray(inv, F32).astype(BF16)
    rb0 = row0 // L

    def uspec(part):
        return pl.BlockSpec((L, tc), lambda b, j: (rb0 + b, part * nc + j))

    def wspec(part):
        return pl.BlockSpec((3, tc), lambda b, j: (0, part * nc + j))

    def bspec(part):
        return pl.BlockSpec((1, tc), lambda b, j: (0, part * nc + j))

    cb = conv_b.reshape(1, 3 * D_MODEL)
    return pl.pallas_call(
        functools.partial(_hyena_body, L=L),
        grid=(nseq, nc),
        in_specs=[uspec(0), uspec(1), uspec(2), wspec(0), wspec(1), wspec(2), bspec(0), bspec(1), bspec(2),
                  pl.BlockSpec((2 * L, tc), lambda b, j: (0, j)),
                  pl.BlockSpec((1, tc), lambda b, j: (0, j)),
                  pl.BlockSpec((2 * L, L), lambda b, j: (0, 0)),
                  pl.BlockSpec((L, 2 * L), lambda b, j: (0, 0))],
        out_specs=pl.BlockSpec((L, tc), lambda b, j: (b, j)),
        out_shape=jax.ShapeDtypeStruct((nseq * L, D_MODEL), BF16),
        compiler_params=_params(2, 40 << 20),
        name="hyena_mix",
    )(u, u, u, conv_w, conv_w, conv_w, cb, cb, cb, hf, fbias.reshape(1, D_MODEL), fwd, inv)


def _hyena_spectrum(L, filt_params):
    h_circ = _hyena_filter_time(L, *filt_params)
    fwd, _ = _dft_matrices(L)
    return _matmul(jnp.asarray(fwd, F32), h_circ, tm=min(2 * L, 1024), tn=512, out_dtype=F32, name="hyena_spectrum")


def _rwkv_mix_body(h_ref, hp_ref, hn_ref, mu_ref, *o_refs, tm):
    i = pl.program_id(0)
    n_p = T_P // tm
    per = DEC_SEQ // tm
    pos = (i - n_p) % per
    first = jnp.logical_or(i < n_p, pos == 0)
    last = jnp.logical_or(i < n_p, pos == per - 1)
    h = h_ref[...]
    row = lax.broadcasted_iota(jnp.int32, (tm, 1), 0)
    before = jnp.where(first, 0.0, hp_ref[7:8, :])
    after = jnp.where(last, 0.0, hn_ref[0:1, :])
    prev = jnp.where(row == 0, before, pltpu.roll(h, 1, axis=0))
    nxt = jnp.where(row == tm - 1, after, pltpu.roll(h, tm - 1, axis=0))
    xx = 0.5 * (prev + nxt) - h
    for n, o_ref in enumerate(o_refs):
        o_ref[...] = (h + xx * mu_ref[n:n + 1, :]).astype(o_ref.dtype)


def _rwkv_token_mix(h, mu):
    tm = SEQ
    tc = 512
    hb = tm // 8
    nrb = T_ALL // 8
    out = jax.ShapeDtypeStruct((T_ALL, D_MODEL), BF16)
    return pl.pallas_call(
        functools.partial(_rwkv_mix_body, tm=tm),
        grid=(T_ALL // tm, D_MODEL // tc),
        in_specs=[pl.BlockSpec((tm, tc), lambda i, j: (i, j)),
                  pl.BlockSpec((8, tc), lambda i, j: (jnp.maximum(i * hb - 1, 0), j)),
                  pl.BlockSpec((8, tc), lambda i, j: (jnp.minimum((i + 1) * hb, nrb - 1), j)),
                  pl.BlockSpec((6, tc), lambda i, j: (0, j))],
        out_specs=[pl.BlockSpec((tm, tc), lambda i, j: (i, j))] * 6,
        out_shape=[out] * 6,
        compiler_params=_params(2, 16 << 20),
        name="rwkv_token_mix",
    )(h, h, h, mu)


def _tri_inverse(n):
    eye = (lax.broadcasted_iota(jnp.int32, n.shape, 0) == lax.broadcasted_iota(jnp.int32, n.shape, 1)).astype(F32)
    p = eye - n
    q = n
    steps = int(math.log2(RWKV_CHUNK)) - 1
    for _ in range(steps):
        q = jnp.dot(q, q, preferred_element_type=F32, precision=HIGHEST)
        p = jnp.dot(p, eye + q, preferred_element_type=F32, precision=HIGHEST)
    return p


def _rwkv_scan_body(*refs, L, has_init):
    it = iter(refs)
    r_ref, k_ref, v_ref, wlf_ref, wlb_ref, alf_ref, alb_ref, kk_ref, ka_ref = (next(it) for _ in range(9))
    s0_ref = next(it) if has_init else None
    yf_ref, yb_ref, sfin_ref = next(it), next(it), next(it)
    C = RWKV_CHUNK
    HD = RWKV_HEAD_DIM
    nc = L // C
    lane = lax.broadcasted_iota(jnp.int32, (1, LANE), 1)
    head0 = lane < HD
    ones_bd = ((lax.broadcasted_iota(jnp.int32, (LANE, LANE), 0) // HD)
               == (lax.broadcasted_iota(jnp.int32, (LANE, LANE), 1) // HD)).astype(F32)
    ci = lax.broadcasted_iota(jnp.int32, (C, C), 0)
    cj = lax.broadcasted_iota(jnp.int32, (C, C), 1)
    bi = lax.broadcasted_iota(jnp.int32, (2 * C, 2 * C), 0) % C
    bj = lax.broadcasted_iota(jnp.int32, (2 * C, 2 * C), 1) % C
    kk_w = kk_ref[...]
    ka_w = ka_ref[...]

    def stack(x):
        return jnp.concatenate([jnp.where(head0, x, 0.0), jnp.where(head0, 0.0, x)], axis=0)

    def chunk(rows, wl_ref, al_ref, s, reverse):
        r = r_ref[rows, :]
        k = k_ref[rows, :]
        v = v_ref[rows, :]
        wl = wl_ref[rows, :]
        al = al_ref[rows, :]
        sp = jnp.maximum(-wl, 0.0) + jnp.log(1.0 + jnp.exp(-jnp.abs(wl)))
        lw = -jnp.exp(-sp - 0.5)
        a = jax.nn.sigmoid(al)
        kkr = k * kk_w
        kk = kkr * lax.rsqrt(jnp.dot(kkr * kkr, ones_bd, preferred_element_type=F32) + 1e-12)
        kd = k * (1.0 + (a - 1.0) * ka_w)
        b = kk * a
        tri = (cj >= ci) if reverse else (cj <= ci)
        g_inc = jnp.dot(tri.astype(F32), lw, preferred_element_type=F32, precision=HIGHEST)
        g_end = g_inc[0:1, :] if reverse else g_inc[C - 1:C, :]
        e_pos = jnp.exp(g_inc)
        e_neg = jnp.exp(-g_inc)
        e_end = jnp.exp(g_end - g_inc)
        lk = stack(kk * jnp.exp(g_inc - lw))
        lr = stack(r * e_pos)
        bs = stack(b * e_neg)
        ks = stack(kd * e_neg)
        vs = stack(v)
        big = _dot_nt(jnp.concatenate([lk, lr], axis=0), jnp.concatenate([bs, ks], axis=0), precision=HIGHEST)
        strict = (bj > bi) if reverse else (bj < bi)
        incl = (bj >= bi) if reverse else (bj <= bi)
        n_bd = jnp.where(strict, big[:2 * C, :2 * C], 0.0)
        m_dk = jnp.where(strict, big[:2 * C, 2 * C:], 0.0)
        a_rb = jnp.where(incl, big[2 * C:, :2 * C], 0.0)
        a_rd = jnp.where(incl, big[2 * C:, 2 * C:], 0.0)
        t_inv = _tri_inverse(n_bd)
        xs = _dot_nt(lk, s, precision=HIGHEST)
        us = -jnp.dot(t_inv, xs + jnp.dot(m_dk, vs, preferred_element_type=F32, precision=HIGHEST),
                      preferred_element_type=F32, precision=HIGHEST)
        ys = (_dot_nt(lr, s, precision=HIGHEST)
              + jnp.dot(a_rb, us, preferred_element_type=F32, precision=HIGHEST)
              + jnp.dot(a_rd, vs, preferred_element_type=F32, precision=HIGHEST))
        s_new = (s * jnp.exp(g_end)
                 + _dot_tn(us, stack(b * e_end), precision=HIGHEST)
                 + _dot_tn(vs, stack(kd * e_end), precision=HIGHEST))
        return ys[:C] + ys[C:], s_new

    def init(d):
        if not has_init:
            return jnp.zeros((LANE, LANE), F32)
        z = jnp.zeros((HD, HD), F32)
        return jnp.concatenate([jnp.concatenate([s0_ref[d, 0], z], axis=1),
                                jnp.concatenate([z, s0_ref[d, 1]], axis=1)], axis=0)

    def step(c, carry):
        s_f, s_b = carry
        rf = pl.ds(pl.multiple_of(c * C, C), C)
        rb = pl.ds(pl.multiple_of((nc - 1 - c) * C, C), C)
        y_f, s_f = chunk(rf, wlf_ref, alf_ref, s_f, False)
        y_b, s_b = chunk(rb, wlb_ref, alb_ref, s_b, True)
        yf_ref[rf, :] = y_f
        yb_ref[rb, :] = y_b
        return s_f, s_b

    s_f, s_b = lax.fori_loop(0, nc, step, (init(0), init(1)))
    for d, s in enumerate((s_f, s_b)):
        sfin_ref[d, 0] = s[:HD, :HD]
        sfin_ref[d, 1] = s[HD:, HD:]


def _rwkv_scan(r, k, v, wl_f, wl_b, al_f, al_b, k_k, k_a, s0, row0, nseq, L):
    rb0 = row0 // L
    seq = lambda b, hp: (rb0 + b, hp)
    blk = pl.BlockSpec((L, LANE), seq)
    vec = pl.BlockSpec((1, LANE), lambda b, hp: (0, hp))
    st = pl.BlockSpec((None, 2, 2, RWKV_HEAD_DIM, RWKV_HEAD_DIM), lambda b, hp: (b, 0, hp, 0, 0))
    has_init = s0 is not None
    in_specs = [blk] * 7 + [vec, vec] + ([st] if has_init else [])
    args = [r, k, v, wl_f, wl_b, al_f, al_b, k_k.reshape(1, D_MODEL), k_a.reshape(1, D_MODEL)]
    if has_init:
        args.append(s0)
    yshape = jax.ShapeDtypeStruct((nseq * L, D_MODEL), F32)
    return pl.pallas_call(
        functools.partial(_rwkv_scan_body, L=L, has_init=has_init),
        grid=(nseq, RWKV_HEADS // 2),
        in_specs=in_specs,
        out_specs=[pl.BlockSpec((L, LANE), lambda b, hp: (b, hp))] * 2 + [st],
        out_shape=[yshape, yshape,
                   jax.ShapeDtypeStruct((nseq, 2, RWKV_HEADS, RWKV_HEAD_DIM, RWKV_HEAD_DIM), F32)],
        compiler_params=_params(2, 32 << 20),
        name="rwkv_scan",
    )(*args)


def _rwkv_post_body(yf_ref, yb_ref, r_ref, k_ref, v_ref, alf_ref, alb_ref, g_ref,
                    ka_ref, rk_ref, lnw_ref, lnb_ref, o_ref, *, tc):
    HD = RWKV_HEAD_DIM
    same_head = ((lax.broadcasted_iota(jnp.int32, (LANE, LANE), 0) // HD)
                 == (lax.broadcasted_iota(jnp.int32, (LANE, LANE), 1) // HD))
    ones_bd = same_head.astype(F32)
    mean_bd = ones_bd * (1.0 / HD)
    for c in range(tc // LANE):
        sl = slice(c * LANE, (c + 1) * LANE)
        y = yf_ref[:, sl] + yb_ref[:, sl]
        mean = jnp.dot(y, mean_bd, preferred_element_type=F32, precision=HIGHEST)
        yc = y - mean
        var = jnp.dot(yc * yc, mean_bd, preferred_element_type=F32, precision=HIGHEST)
        yn = yc * lax.rsqrt(var + RWKV_GN_EPS) * lnw_ref[:, sl] + lnb_ref[:, sl]
        k = k_ref[:, sl]
        ka = ka_ref[:, sl]
        kd_sum = (k * (1.0 + (jax.nn.sigmoid(alf_ref[:, sl]) - 1.0) * ka)
                  + k * (1.0 + (jax.nn.sigmoid(alb_ref[:, sl]) - 1.0) * ka))
        bonus = jnp.dot(r_ref[:, sl] * kd_sum * rk_ref[:, sl], ones_bd,
                        preferred_element_type=F32, precision=HIGHEST) * v_ref[:, sl]
        o_ref[:, sl] = ((yn + bonus) * g_ref[:, sl]).astype(o_ref.dtype)


def _rwkv_post(yf, yb, r, k, v, al_f, al_b, g, k_a, r_k, ln_w, ln_b):
    tm, tc = 512, 512
    blk = pl.BlockSpec((tm, tc), lambda i, j: (i, j))
    vec = pl.BlockSpec((1, tc), lambda i, j: (0, j))
    row = lambda x: x.reshape(1, D_MODEL)
    return pl.pallas_call(
        functools.partial(_rwkv_post_body, tc=tc),
        grid=(T_ALL // tm, D_MODEL // tc),
        in_specs=[blk] * 8 + [vec] * 4,
        out_specs=blk,
        out_shape=jax.ShapeDtypeStruct((T_ALL, D_MODEL), BF16),
        compiler_params=_params(2, 32 << 20),
        name="rwkv_post",
    )(yf, yb, r, k, v, al_f, al_b, g, row(k_a), row(r_k), row(ln_w), row(ln_b))


def _pad_lora(w_down, w_up):
    inner = w_down.shape[-1]
    pad = (-inner) % LANE
    return (jnp.pad(w_down, ((0, 0), (0, pad))), jnp.pad(w_up, ((0, pad), (0, 0))))


def kernel(x_prompt, x_sample, cache_attn_k, cache_attn_v, state_rwkv, c, c_ctx, ada_w, ada_b, norm1_g, norm2_g, mlp_w1, mlp_w2, attn_wqkv, attn_wo, attn_sink, hy_w_in, hy_conv_w, hy_conv_b, hy_f_w1, hy_f_b1, hy_f_w2, hy_f_b2, hy_f_w3, hy_f_b3, hy_f_freq, hy_f_wout, hy_bias, hy_w_out, rw_mu, rw_wr, rw_wk, rw_wv, rw_wo, rw_w0, rw_w1, rw_w2, rw_a0, rw_a1, rw_a2, rw_g1, rw_g2, rw_k_k, rw_k_a, rw_r_k, rw_ln_w, rw_ln_b, final_norm_g):
    D = D_MODEL
    x = jnp.concatenate([x_prompt.reshape(T_P, D), x_sample.reshape(T_S, D)], axis=0)
    cond = jnp.concatenate([c_ctx[None, :], c, jnp.zeros((N_COND - 1 - DEC_BATCH, D), F32)], axis=0)
    mod = _ada_table(cond, ada_w, ada_b).reshape(DEPTH, N_COND, 6, 1, D)
    kv_w = ATTN_KV_HEADS * ATTN_HEAD_DIM
    ck_all = cache_attn_k.reshape(DEC_BATCH, -1, PAST_LEN, kv_w)
    cv_all = cache_attn_v.reshape(DEC_BATCH, -1, PAST_LEN, kv_w)
    dense = dict(tm=1024, tn=512)
    new_k, new_v, new_s = [], [], []
    for i in range(DEPTH):
        kind, j = i % N_MIXERS, i // N_MIXERS
        sh1, sc1, gt1, sh2, sc2, gt2 = (mod[i, :, n] for n in range(6))
        if kind == 0:
            h = _norm_mod(x, norm1_g[i], sh1, sc1, BF16)
            qkv = _matmul(h, attn_wqkv[j], out_dtype=F32, **dense)
            qd = ATTN_HEADS * ATTN_HEAD_DIM
            new_k.append(qkv[:T_P, qd:qd + kv_w].reshape(BATCH, SEQ, ATTN_KV_HEADS, ATTN_HEAD_DIM))
            new_v.append(qkv[:T_P, qd + kv_w:].reshape(BATCH, SEQ, ATTN_KV_HEADS, ATTN_HEAD_DIM))
            o = jnp.concatenate([_attn_context(qkv, attn_sink[j]),
                                 _attn_latent(qkv, ck_all, cv_all, j, attn_sink[j])], axis=0)
            x = _matmul(o, attn_wo[j], res=x, gate=gt1, **dense)
        elif kind == 1:
            h = _norm_mod(x, norm1_g[i], sh1, sc1, BF16)
            u = _matmul(h, hy_w_in[j], out_dtype=F32, **dense)
            filt = (hy_f_w1[j], hy_f_b1[j], hy_f_w2[j], hy_f_b2[j], hy_f_w3[j], hy_f_b3[j],
                    hy_f_freq[j], hy_f_wout[j])
            o = jnp.concatenate(
                [_hyena_mix(u, 0, BATCH, SEQ, hy_conv_w[j], hy_conv_b[j], _hyena_spectrum(SEQ, filt), hy_bias[j]),
                 _hyena_mix(u, T_P, DEC_BATCH, DEC_SEQ, hy_conv_w[j], hy_conv_b[j],
                            _hyena_spectrum(DEC_SEQ, filt), hy_bias[j])], axis=0)
            x = _matmul(o, hy_w_out[j], res=x, gate=gt1, **dense)
        else:
            h = _norm_mod(x, norm1_g[i], sh1, sc1, F32)
            xr, xw, xk, xv, xa, xg = _rwkv_token_mix(h, rw_mu[j])
            r = _matmul(xr, rw_wr[j], **dense)
            k = _matmul(xk, rw_wk[j], **dense)
            v = _matmul(xv, rw_wv[j], **dense)
            g = _matmul(_matmul(xg, rw_g1[j], tm=1024, tn=256, out_dtype=BF16, act="sigmoid"), rw_g2[j], **dense)
            wl, al = [], []
            for d in range(2):
                w1p, w2p = _pad_lora(rw_w1[j, d], rw_w2[j, d])
                a1p, a2p = _pad_lora(rw_a1[j, d], rw_a2[j, d])
                wl.append(_matmul(_matmul(xw, w1p, tm=1024, tn=LANE, out_dtype=BF16, act="tanh"), w2p,
                                  bias=rw_w0[j, d], **dense))
                al.append(_matmul(_matmul(xa, a1p, tm=1024, tn=LANE, out_dtype=BF16), a2p,
                                  bias=rw_a0[j, d], **dense))
            scan = (r, k, v, wl[0], wl[1], al[0], al[1], rw_k_k[j], rw_k_a[j])
            yf_p, yb_p, s_fin = _rwkv_scan(*scan, None, 0, BATCH, SEQ)
            yf_s, yb_s, _ = _rwkv_scan(*scan, state_rwkv[:, j], T_P, DEC_BATCH, DEC_SEQ)
            new_s.append(s_fin)
            o = _rwkv_post(jnp.concatenate([yf_p, yf_s], axis=0), jnp.concatenate([yb_p, yb_s], axis=0),
                           r, k, v, al[0], al[1], g, rw_k_a[j], rw_r_k[j], rw_ln_w[j], rw_ln_b[j])
            x = _matmul(o, rw_wo[j], res=x, gate=gt1, **dense)
        h = _norm_mod(x, norm2_g[i], sh2, sc2, BF16)
        hid = _matmul(h, mlp_w1[i], out_dtype=BF16, act="relu2", name="mlp_up", **dense)
        x = _matmul(hid, mlp_w2[i].astype(BF16), tm=1024, tn=1024, tk=2048, res=x, gate=gt2, name="mlp_down")
    y = _norm_mod(x, final_norm_g, mod[0, :, 0], mod[0, :, 0], F32, modulate=False)
    return (y[:T_P].reshape(BATCH, SEQ, D), y[T_P:].reshape(DEC_BATCH, DEC_SEQ, D),
            jnp.stack(new_k, axis=1), jnp.stack(new_v, axis=1), jnp.stack(new_s, axis=1))
```

```python
import functools
import math

import numpy as np
import jax
import jax.numpy as jnp
from jax import lax
from jax.experimental import pallas as pl
from jax.experimental.pallas import tpu as pltpu

F32 = jnp.float32
BF16 = jnp.bfloat16

D_MODEL = 2048
BATCH = 16
SEQ = 256
DEPTH = 4
DEC_BATCH = 8
DEC_SEQ = 1024
PAST_LEN = 512
GRID_W = 64
N_MIXERS = 3

ATTN_HEADS = 16
ATTN_KV_HEADS = 4
ATTN_HEAD_DIM = 128
ATTN_GROUP = ATTN_HEADS // ATTN_KV_HEADS
WINDOW = 128
ATTN_BLOCK = 128
ROPE_BASE = 10000.0
NEG_INF = -1e30

HYENA_EMB = 33
HYENA_TARGET = 1e-2
HYENA_FAST_DECAY = 0.3
HYENA_SLOW_DECAY = 1.5

RWKV_HEAD_DIM = 64
RWKV_HEADS = D_MODEL // RWKV_HEAD_DIM
RWKV_GN_EPS = 64e-5
RWKV_CHUNK = 64
NORM_EPS = 1e-6
D_FF = 4 * D_MODEL

T_P = BATCH * SEQ
T_S = DEC_BATCH * DEC_SEQ
T_ALL = T_P + T_S
N_COND = 16

LANE = 128
V7X_VMEM_LIMIT = 56 * 1024 * 1024


def _params(n_axes, vmem_bytes):
    return pltpu.CompilerParams(dimension_semantics=("arbitrary",) * n_axes,
                                vmem_limit_bytes=int(min(max(vmem_bytes, 16 * 2**20), V7X_VMEM_LIMIT)))


def _cond_row(i, tm):
    n_p = T_P // tm
    per = DEC_SEQ // tm
    return jnp.where(i < n_p, 0, 1 + (i - n_p) // per)


def _dot_nt(a, b, **kw):
    return lax.dot_general(a, b, (((1,), (1,)), ((), ())), preferred_element_type=F32, **kw)


def _mm_body(*refs, nk, cast_w, act, has_bias, has_res):
    it = iter(refs)
    a_ref = next(it)
    w_ref = next(it)
    bias_ref = next(it) if has_bias else None
    res_ref = next(it) if has_res else None
    gate_ref = next(it) if has_res else None
    o_ref = next(it)
    acc_ref = next(it) if nk > 1 else None
    wb_ref = next(it) if cast_w else None

    if cast_w:
        @pl.when(pl.program_id(1) == 0)
        def _():
            wb_ref[...] = w_ref[...].astype(BF16)
        w = wb_ref[...]
    else:
        w = w_ref[...].astype(BF16)
    p = jnp.dot(a_ref[...].astype(BF16), w, preferred_element_type=F32)

    def finish(acc):
        if has_bias:
            acc = acc + bias_ref[...]
        if act == "relu2":
            acc = jnp.square(jnp.maximum(acc, 0.0))
        elif act == "tanh":
            acc = jnp.tanh(acc)
        elif act == "sigmoid":
            acc = jax.nn.sigmoid(acc)
        if has_res:
            acc = res_ref[...] + gate_ref[...] * acc
        o_ref[...] = acc.astype(o_ref.dtype)

    if nk == 1:
        finish(p)
    else:
        k = pl.program_id(2)

        @pl.when(k == 0)
        def _():
            acc_ref[...] = p

        @pl.when(k > 0)
        def _():
            acc_ref[...] += p

        @pl.when(k == nk - 1)
        def _():
            finish(acc_ref[...])


def _matmul(a, w, *, tm, tn, tk=None, out_dtype=F32, act=None, bias=None, res=None, gate=None, name="matmul"):
    M, K = a.shape
    N = w.shape[1]
    tk = K if tk is None else tk
    nk = K // tk
    assert M % tm == 0 and N % tn == 0 and K % tk == 0
    cast_w = nk == 1 and w.dtype != BF16
    has_bias = bias is not None
    has_res = res is not None
    in_specs = [pl.BlockSpec((tm, tk), lambda j, i, k: (i, k)),
                pl.BlockSpec((tk, tn), lambda j, i, k: (k, j))]
    args = [a, w]
    if has_bias:
        in_specs.append(pl.BlockSpec((1, tn), lambda j, i, k: (0, j)))
        args.append(bias.reshape(1, N))
    if has_res:
        in_specs.append(pl.BlockSpec((tm, tn), lambda j, i, k: (i, j)))
        in_specs.append(pl.BlockSpec((None, 1, tn), lambda j, i, k: (_cond_row(i, tm), 0, j)))
        args += [res, gate]
    scratch = []
    if nk > 1:
        scratch.append(pltpu.VMEM((tm, tn), F32))
    if cast_w:
        scratch.append(pltpu.VMEM((tk, tn), BF16))
    vmem = (2 * (tm * tk * a.dtype.itemsize + tk * tn * w.dtype.itemsize + tm * tn * jnp.dtype(out_dtype).itemsize
                 + (tm * tn * 4 if has_res else 0)) + 3 * tm * tn * 4 + tk * tn * 2 + tm * tk * 2 + (4 << 20))
    return pl.pallas_call(
        functools.partial(_mm_body, nk=nk, cast_w=cast_w, act=act, has_bias=has_bias, has_res=has_res),
        grid=(N // tn, M // tm, nk),
        in_specs=in_specs,
        out_specs=pl.BlockSpec((tm, tn), lambda j, i, k: (i, j)),
        out_shape=jax.ShapeDtypeStruct((M, N), out_dtype),
        scratch_shapes=scratch,
        compiler_params=_params(3, vmem),
        name=name,
    )(*args)


def _ada_body(c_ref, w_ref, b_ref, o_ref):
    c = c_ref[...]
    s = (c * jax.nn.sigmoid(c)).astype(BF16)
    o_ref[...] = jnp.dot(s, w_ref[...].astype(BF16), preferred_element_type=F32) + b_ref[...]


def _ada_table(cond, ada_w, ada_b):
    tn = 1024
    n6 = 6 * D_MODEL
    return pl.pallas_call(
        _ada_body,
        grid=(DEPTH, n6 // tn),
        in_specs=[pl.BlockSpec((N_COND, D_MODEL), lambda l, j: (0, 0)),
                  pl.BlockSpec((None, D_MODEL, tn), lambda l, j: (l, 0, j)),
                  pl.BlockSpec((None, 1, tn), lambda l, j: (l, 0, j))],
        out_specs=pl.BlockSpec((None, N_COND, tn), lambda l, j: (l, 0, j)),
        out_shape=jax.ShapeDtypeStruct((DEPTH, N_COND, n6), F32),
        compiler_params=_params(2, 2 * D_MODEL * tn * 4 + D_MODEL * tn * 2 + (4 << 20)),
        name="ada_table",
    )(cond, ada_w, ada_b.reshape(DEPTH, 1, n6))


def _norm_body(x_ref, g_ref, sh_ref, sc_ref, o_ref, *, modulate):
    x = x_ref[...]
    y = x * lax.rsqrt(jnp.mean(x * x, axis=-1, keepdims=True) + NORM_EPS) * g_ref[...]
    if modulate:
        y = y * (1.0 + sc_ref[...]) + sh_ref[...]
    o_ref[...] = y.astype(o_ref.dtype)


def _norm_mod(x, g, shift, scale, out_dtype, modulate=True):
    tm = 512
    cond_spec = pl.BlockSpec((None, 1, D_MODEL), lambda i: (_cond_row(i, tm), 0, 0))
    return pl.pallas_call(
        functools.partial(_norm_body, modulate=modulate),
        grid=(T_ALL // tm,),
        in_specs=[pl.BlockSpec((tm, D_MODEL), lambda i: (i, 0)),
                  pl.BlockSpec((1, D_MODEL), lambda i: (0, 0)),
                  cond_spec, cond_spec],
        out_specs=pl.BlockSpec((tm, D_MODEL), lambda i: (i, 0)),
        out_shape=jax.ShapeDtypeStruct((T_ALL, D_MODEL), out_dtype),
        compiler_params=_params(1, 6 * tm * D_MODEL * 4 + (4 << 20)),
        name="norm_mod",
    )(x, g.reshape(1, D_MODEL), shift, scale)


def _softmax_sink_pv(parts, sink_col):
    m = sink_col
    for s, _ in parts:
        m = jnp.maximum(m, jnp.max(s, axis=-1, keepdims=True))
    denom = jnp.exp(sink_col - m)
    out = None
    for s, v in parts:
        p = jnp.exp(s - m)
        denom = denom + jnp.sum(p, axis=-1, keepdims=True)
        pv = jnp.dot(p.astype(BF16), v, preferred_element_type=F32)
        out = pv if out is None else out + pv
    return out / denom


def _attn_ctx_body(sink_ref, q_ref, k_ref, v_ref, o_ref):
    h = pl.program_id(1)
    scale = ATTN_HEAD_DIM ** -0.5
    k = k_ref[...].astype(BF16)
    v = v_ref[...].astype(BF16)
    for g in range(ATTN_GROUP):
        q = q_ref[:, g * ATTN_HEAD_DIM:(g + 1) * ATTN_HEAD_DIM].astype(BF16)
        s = _dot_nt(q, k) * scale
        sink = jnp.full((SEQ, 1), sink_ref[h * ATTN_GROUP + g], F32)
        o = _softmax_sink_pv([(s, v)], sink)
        o_ref[:, g * ATTN_HEAD_DIM:(g + 1) * ATTN_HEAD_DIM] = o.astype(o_ref.dtype)


def _attn_context(qkv, sink):
    gw = ATTN_GROUP * ATTN_HEAD_DIM
    kcol = ATTN_HEADS * ATTN_HEAD_DIM // ATTN_HEAD_DIM
    vcol = kcol + ATTN_KV_HEADS
    return pl.pallas_call(
        _attn_ctx_body,
        grid=(BATCH, ATTN_KV_HEADS),
        in_specs=[pl.BlockSpec(memory_space=pltpu.SMEM),
                  pl.BlockSpec((SEQ, gw), lambda b, h: (b, h)),
                  pl.BlockSpec((SEQ, ATTN_HEAD_DIM), lambda b, h: (b, kcol + h)),
                  pl.BlockSpec((SEQ, ATTN_HEAD_DIM), lambda b, h: (b, vcol + h))],
        out_specs=pl.BlockSpec((SEQ, gw), lambda b, h: (b, h)),
        out_shape=jax.ShapeDtypeStruct((T_ALL, ATTN_HEADS * ATTN_HEAD_DIM), BF16),
        compiler_params=_params(2, 16 << 20),
        name="attn_context",
    )(sink, qkv, qkv, qkv)


def _rope(x, cos, sin_lo, sin_hi):
    return (x * cos + pltpu.roll(x, ATTN_HEAD_DIM - 32, axis=1) * sin_lo
            + pltpu.roll(x, 32, axis=1) * sin_hi)


def _attn_lat_body(sink_ref, q_ref, k_ref, v_ref, ck_ref, cv_ref, cos_ref, slo_ref, shi_ref, dst_ref,
                   o_ref, kr_ref):
    del dst_ref
    h = pl.program_id(1)
    scale = ATTN_HEAD_DIM ** -0.5
    blk = ATTN_BLOCK
    nb = DEC_SEQ // blk
    rows = ATTN_GROUP * blk
    kr_ref[...] = _rope(k_ref[...], cos_ref[...], slo_ref[...], shi_ref[...]).astype(BF16)
    ck = ck_ref[...].astype(BF16)
    cv = cv_ref[...].astype(BF16)
    row_head = lax.broadcasted_iota(jnp.int32, (rows, 1), 0) // blk
    sink = jnp.zeros((rows, 1), F32)
    for g in range(ATTN_GROUP):
        sink = jnp.where(row_head == g, sink_ref[h * ATTN_GROUP + g], sink)
    q_in_blk = lax.broadcasted_iota(jnp.int32, (rows, 3 * blk), 0) % blk
    k_in_band = lax.broadcasted_iota(jnp.int32, (rows, 3 * blk), 1)

    def block(j, carry):
        r0 = pl.multiple_of(j * blk, blk)
        k0 = pl.multiple_of(jnp.clip(j - 1, 0, nb - 3) * blk, blk)
        cos = cos_ref[pl.ds(r0, blk), :]
        slo = slo_ref[pl.ds(r0, blk), :]
        shi = shi_ref[pl.ds(r0, blk), :]
        qs = jnp.concatenate(
            [_rope(q_ref[pl.ds(r0, blk), g * ATTN_HEAD_DIM:(g + 1) * ATTN_HEAD_DIM], cos, slo, shi)
             for g in range(ATTN_GROUP)], axis=0).astype(BF16)
        kl = kr_ref[pl.ds(k0, 3 * blk), :]
        vl = v_ref[pl.ds(k0, 3 * blk), :].astype(BF16)
        rel = (k0 + k_in_band) - (r0 + q_in_blk)
        s_loc = jnp.where(jnp.abs(rel) <= WINDOW, _dot_nt(qs, kl) * scale, NEG_INF)
        s_ctx = _dot_nt(qs, ck) * scale
        o = _softmax_sink_pv([(s_loc, vl), (s_ctx, cv)], sink)
        for g in range(ATTN_GROUP):
            o_ref[pl.ds(r0, blk), g * ATTN_HEAD_DIM:(g + 1) * ATTN_HEAD_DIM] = (
                o[g * blk:(g + 1) * blk].astype(o_ref.dtype))
        return carry

    lax.fori_loop(0, nb, block, 0)


def _rope_tables():
    t = np.arange(DEC_SEQ)
    half = ATTN_HEAD_DIM // 2
    inv = ROPE_BASE ** (-np.arange(0, half, 2, dtype=np.float64) / half)
    ang_r = (t // GRID_W).astype(np.float64)[:, None] * inv
    ang_c = (t % GRID_W).astype(np.float64)[:, None] * inv
    ang = np.concatenate([ang_r, ang_r, ang_c, ang_c], axis=1)
    lane = np.arange(ATTN_HEAD_DIM)
    first = (lane % 64) < 32
    cos = np.cos(ang)
    sin = np.sin(ang)
    sin_lo = np.where(first, -sin, 0.0)
    sin_hi = np.where(first, 0.0, sin)
    return (jnp.asarray(cos, F32), jnp.asarray(sin_lo, F32), jnp.asarray(sin_hi, F32))


def _attn_latent(qkv, cache_k, cache_v, layer, sink, dst):
    gw = ATTN_GROUP * ATTN_HEAD_DIM
    kcol = ATTN_HEADS
    vcol = kcol + ATTN_KV_HEADS
    rb0 = T_P // DEC_SEQ
    cos, slo, shi = _rope_tables()
    tab = pl.BlockSpec((DEC_SEQ, ATTN_HEAD_DIM), lambda b, h: (0, 0))
    cache_spec = pl.BlockSpec((None, None, PAST_LEN, ATTN_HEAD_DIM), lambda b, h: (b, layer, 0, h))
    return pl.pallas_call(
        _attn_lat_body,
        grid=(DEC_BATCH, ATTN_KV_HEADS),
        in_specs=[pl.BlockSpec(memory_space=pltpu.SMEM),
                  pl.BlockSpec((DEC_SEQ, gw), lambda b, h: (rb0 + b, h)),
                  pl.BlockSpec((DEC_SEQ, ATTN_HEAD_DIM), lambda b, h: (rb0 + b, kcol + h)),
                  pl.BlockSpec((DEC_SEQ, ATTN_HEAD_DIM), lambda b, h: (rb0 + b, vcol + h)),
                  cache_spec, cache_spec, tab, tab, tab, pl.BlockSpec(memory_space=pl.ANY)],
        out_specs=pl.BlockSpec((DEC_SEQ, gw), lambda b, h: (rb0 + b, h)),
        out_shape=jax.ShapeDtypeStruct(dst.shape, dst.dtype),
        input_output_aliases={9: 0},
        scratch_shapes=[pltpu.VMEM((DEC_SEQ, ATTN_HEAD_DIM), BF16)],
        compiler_params=_params(2, 32 << 20),
        name="attn_latent",
    )(sink, qkv, qkv, qkv, cache_k, cache_v, cos, slo, shi, dst)


def _dft_matrices(L):
    N = 2 * L
    p = np.arange(N)[:, None]
    n = np.arange(N)[None, :]
    f = np.where(p < L, p, p - L)
    ang = 2.0 * np.pi * ((f * n) % N) / N
    fwd = np.where(p < L, np.cos(ang), np.where(p == L, np.cos(np.pi * n), -np.sin(ang)))
    t = np.arange(L)[:, None]
    pp = np.arange(N)[None, :]
    ff = np.where(pp < L, pp, pp - L)
    ang2 = 2.0 * np.pi * ((ff * t) % N) / N
    inv = np.where(pp < L, np.where(pp == 0, 1.0, 2.0) * np.cos(ang2),
                   np.where(pp == L, np.cos(np.pi * t), -2.0 * np.sin(ang2))) / N
    return fwd, inv


def _hyena_filter_time(L, f_w1, f_b1, f_w2, f_b2, f_w3, f_b3, freq, f_wout):
    t = jnp.linspace(0.0, 1.0, L, dtype=F32)[:, None]
    bands = (HYENA_EMB - 1) // 2
    w = 2 * math.pi * jnp.arange(L, dtype=F32)[:, None] / L
    fr = jnp.linspace(1e-4, bands - 1, bands, dtype=F32)[None, :]
    z = jnp.concatenate([t, jnp.cos(fr * w), -jnp.sin(fr * w)], axis=-1)
    hdn = jnp.sin(freq * (z @ f_w1 + f_b1))
    hdn = jnp.sin(freq * (hdn @ f_w2 + f_b2))
    hdn = jnp.sin(freq * (hdn @ f_w3 + f_b3))
    filt = (hdn @ f_wout).reshape(L, 2, D_MODEL)
    deltas = jnp.linspace(math.log(HYENA_TARGET) / HYENA_SLOW_DECAY,
                          math.log(HYENA_TARGET) / HYENA_FAST_DECAY, D_MODEL, dtype=F32)
    filt = filt * jnp.exp(-t * jnp.abs(deltas))[:, None, :]
    return jnp.concatenate([filt[:, 0], jnp.zeros((1, D_MODEL), F32), filt[:0:-1, 1]], axis=0)


def _hyena_body(x0_ref, x1_ref, v_ref, cw0_ref, cw1_ref, cwv_ref, cb0_ref, cb1_ref, cbv_ref,
                hf_ref, fb_ref, fwd_ref, inv_ref, *rest, L):
    o_ref = rest[-1]
    row = lax.broadcasted_iota(jnp.int32, (L, 1), 0)

    def conv3(u_ref, w_ref, b_ref):
        u = u_ref[...]
        prev = jnp.where(row == 0, 0.0, pltpu.roll(u, 1, axis=0))
        nxt = jnp.where(row == L - 1, 0.0, pltpu.roll(u, L - 1, axis=0))
        return prev * w_ref[0:1, :] + u * w_ref[1:2, :] + nxt * w_ref[2:3, :] + b_ref[...]

    x0 = conv3(x0_ref, cw0_ref, cb0_ref)
    x1 = conv3(x1_ref, cw1_ref, cb1_ref)
    v = conv3(v_ref, cwv_ref, cbv_ref)
    z = v * x1
    zf = jnp.dot(fwd_ref[...], z.astype(BF16), preferred_element_type=F32)
    zt, zb = zf[:L], zf[L:]
    ht, hb = hf_ref[0:L, :], hf_ref[L:2 * L, :]
    first = row == 0
    yt = zt * ht - jnp.where(first, 0.0, zb * hb)
    yb = jnp.where(first, zb * hb, zt * hb + zb * ht)
    yf = jnp.concatenate([yt, yb], axis=0).astype(BF16)
    y = jnp.dot(inv_ref[...], yf, preferred_element_type=F32) + z * fb_ref[...]
    o_ref[...] = (y * x0).astype(o_ref.dtype)


def _hyena_mix(u, row0, nseq, L, conv_w, conv_b, hf, fbias, dst=None):
    tc = 256
    nc = D_MODEL // tc
    fwd, inv = _dft_matrices(L)
    fwd = jnp.asarray(fwd[:, :L], F32).astype(BF16)
    inv = jnp.asarray(inv, F32).astype(BF16)
    rb0 = row0 // L

    def uspec(part):
        return pl.BlockSpec((L, tc), lambda b, j: (rb0 + b, part * nc + j))

    def wspec(part):
        return pl.BlockSpec((3, tc), lambda b, j: (0, part * nc + j))

    def bspec(part):
        return pl.BlockSpec((1, tc), lambda b, j: (0, part * nc + j))

    cb = conv_b.reshape(1, 3 * D_MODEL)
    in_specs = [uspec(0), uspec(1), uspec(2), wspec(0), wspec(1), wspec(2), bspec(0), bspec(1), bspec(2),
                pl.BlockSpec((2 * L, tc), lambda b, j: (0, j)),
                pl.BlockSpec((1, tc), lambda b, j: (0, j)),
                pl.BlockSpec((2 * L, L), lambda b, j: (0, 0)),
                pl.BlockSpec((L, 2 * L), lambda b, j: (0, 0))]
    args = [u, u, u, conv_w, conv_w, conv_w, cb, cb, cb, hf, fbias.reshape(1, D_MODEL), fwd, inv]
    aliases = {}
    if dst is not None:
        aliases = {len(args): 0}
        in_specs.append(pl.BlockSpec(memory_space=pl.ANY))
        args.append(dst)
    return pl.pallas_call(
        functools.partial(_hyena_body, L=L),
        grid=(nseq, nc),
        in_specs=in_specs,
        out_specs=pl.BlockSpec((L, tc), lambda b, j: (rb0 + b, j)),
        out_shape=jax.ShapeDtypeStruct((T_ALL, D_MODEL), BF16),
        input_output_aliases=aliases,
        compiler_params=_params(2, 40 << 20),
        name="hyena_mix",
    )(*args)


def _hyena_spectrum(L, filt_params):
    h_circ = _hyena_filter_time(L, *filt_params)
    fwd, _ = _dft_matrices(L)
    return _matmul(jnp.asarray(fwd, F32), h_circ, tm=min(2 * L, 1024), tn=512, out_dtype=F32, name="hyena_spectrum")


def _rwkv_mix_body(h_ref, hp_ref, hn_ref, mu_ref, *o_refs, tm):
    i = pl.program_id(0)
    n_p = T_P // tm
    per = DEC_SEQ // tm
    pos = (i - n_p) % per
    first = jnp.logical_or(i < n_p, pos == 0)
    last = jnp.logical_or(i < n_p, pos == per - 1)
    h = h_ref[...]
    row = lax.broadcasted_iota(jnp.int32, (tm, 1), 0)
    before = jnp.where(first, 0.0, hp_ref[7:8, :])
    after = jnp.where(last, 0.0, hn_ref[0:1, :])
    prev = jnp.where(row == 0, before, pltpu.roll(h, 1, axis=0))
    nxt = jnp.where(row == tm - 1, after, pltpu.roll(h, tm - 1, axis=0))
    xx = 0.5 * (prev + nxt) - h
    for n, o_ref in enumerate(o_refs):
        o_ref[...] = (h + xx * mu_ref[n:n + 1, :]).astype(o_ref.dtype)


def _rwkv_token_mix(h, mu):
    tm = SEQ
    tc = 512
    hb = tm // 8
    nrb = T_ALL // 8
    out = jax.ShapeDtypeStruct((T_ALL, D_MODEL), BF16)
    return pl.pallas_call(
        functools.partial(_rwkv_mix_body, tm=tm),
        grid=(T_ALL // tm, D_MODEL // tc),
        in_specs=[pl.BlockSpec((tm, tc), lambda i, j: (i, j)),
                  pl.BlockSpec((8, tc), lambda i, j: (jnp.maximum(i * hb - 1, 0), j)),
                  pl.BlockSpec((8, tc), lambda i, j: (jnp.minimum((i + 1) * hb, nrb - 1), j)),
                  pl.BlockSpec((6, tc), lambda i, j: (0, j))],
        out_specs=[pl.BlockSpec((tm, tc), lambda i, j: (i, j))] * 6,
        out_shape=[out] * 6,
        compiler_params=_params(2, 16 << 20),
        name="rwkv_token_mix",
    )(h, h, h, mu)


def _bdot(a, b):
    return jnp.dot(a.astype(BF16), b.astype(BF16), preferred_element_type=F32)


TRI_BASE = 8


def _tri_inverse_steps(n):
    m = n.shape[0]
    ri = lax.broadcasted_iota(jnp.int32, n.shape, 0)
    rj = lax.broadcasted_iota(jnp.int32, n.shape, 1)
    eye = (ri == rj).astype(F32)
    nb = jnp.where(ri // TRI_BASE == rj // TRI_BASE, n, 0.0)
    p = eye - nb
    q = _bdot(nb, nb)
    yield
    qp = _bdot(jnp.concatenate([q, p], axis=0), q)
    q, p = qp[:m], p + qp[m:]
    yield
    t = p + _bdot(p, q)
    yield
    s = TRI_BASE
    while s < RWKV_CHUNK:
        off = jnp.where((ri // (2 * s) == rj // (2 * s)) & (ri // s != rj // s), n, 0.0)
        mt = _bdot(off, t)
        yield
        t = t - _bdot(t, mt)
        yield
        s *= 2
    return t


def _interleave(gens):
    results = [None] * len(gens)
    live = list(enumerate(gens))
    while live:
        still = []
        for n, g in live:
            try:
                next(g)
                still.append((n, g))
            except StopIteration as done:
                results[n] = done.value
        live = still
    return results


def _rwkv_scan_body(*refs, L, has_init, has_dst, pairs):
    it = iter(refs)
    r_ref, k_ref, v_ref, wlf_ref, wlb_ref, alf_ref, alb_ref, kk_ref, ka_ref = (next(it) for _ in range(9))
    s0_ref = next(it) if has_init else None
    if has_dst:
        next(it), next(it)
    yf_ref, yb_ref, sfin_ref, s_ref = next(it), next(it), next(it), next(it)
    C = RWKV_CHUNK
    HD = RWKV_HEAD_DIM
    nc = L // C
    lane = lax.broadcasted_iota(jnp.int32, (1, LANE), 1)
    head0 = lane < HD
    ones_bd = ((lax.broadcasted_iota(jnp.int32, (LANE, LANE), 0) // HD)
               == (lax.broadcasted_iota(jnp.int32, (LANE, LANE), 1) // HD)).astype(BF16)
    ci = lax.broadcasted_iota(jnp.int32, (C, C), 0)
    cj = lax.broadcasted_iota(jnp.int32, (C, C), 1)
    bi = lax.broadcasted_iota(jnp.int32, (2 * C, 2 * C), 0) % C
    bj = lax.broadcasted_iota(jnp.int32, (2 * C, 2 * C), 1) % C

    def stack(x):
        return jnp.concatenate([jnp.where(head0, x, 0.0), jnp.where(head0, 0.0, x)], axis=0)

    def chunk(r, k, v, wl, al, kk_w, ka_w, s, reverse):
        sp = jnp.maximum(-wl, 0.0) + jnp.log(1.0 + jnp.exp(-jnp.abs(wl)))
        lw = -jnp.exp(-sp - 0.5)
        a = jax.nn.sigmoid(al)
        kkr = k * kk_w
        kk = kkr * lax.rsqrt(_bdot(kkr * kkr, ones_bd) + 1e-12)
        kd = k * (1.0 + (a - 1.0) * ka_w)
        b = kk * a
        tri = ((cj >= ci) if reverse else (cj <= ci)).astype(BF16)
        lw_hi = lw.astype(BF16)
        lw_lo = (lw - lw_hi.astype(F32)).astype(BF16)
        g_inc = (jnp.dot(tri, lw_hi, preferred_element_type=F32)
                 + jnp.dot(tri, lw_lo, preferred_element_type=F32))
        yield
        g_end = g_inc[0:1, :] if reverse else g_inc[C - 1:C, :]
        e_neg = jnp.exp(-g_inc)
        e_end = jnp.exp(g_end - g_inc)
        lk = stack(kk * jnp.exp(g_inc - lw)).astype(BF16)
        lr = stack(r * jnp.exp(g_inc)).astype(BF16)
        bs = stack(b * e_neg).astype(BF16)
        ks = stack(kd * e_neg).astype(BF16)
        vs = stack(v)
        vs_b = vs.astype(BF16)
        big = _dot_nt(jnp.concatenate([lk, lr], axis=0), jnp.concatenate([bs, ks], axis=0))
        yield
        strict = (bj > bi) if reverse else (bj < bi)
        incl = (bj >= bi) if reverse else (bj <= bi)
        n_bd = jnp.where(strict, big[:2 * C, :2 * C], 0.0)
        m_dk = jnp.where(strict, big[:2 * C, 2 * C:], 0.0)
        a_all = jnp.where(jnp.concatenate([incl, incl], axis=1), big[2 * C:, :], 0.0)
        s_b = s.astype(BF16)
        w = _dot_nt(lk, s_b) + _bdot(m_dk, vs_b)
        ys0 = _dot_nt(lr, s_b)
        t_inv = yield from _tri_inverse_steps(n_bd)
        us = -_bdot(t_inv, w)
        yield
        uv = jnp.concatenate([us, vs], axis=0)
        ys = ys0 + _bdot(a_all, uv)
        s_new = s * jnp.exp(g_end) + _bdot(uv.T, jnp.concatenate([stack(b * e_end), stack(kd * e_end)], axis=0))
        return ys[:C] + ys[C:], s_new

    z = jnp.zeros((HD, HD), F32)
    for d in range(2):
        for p in range(pairs):
            if has_init:
                s_ref[d, p] = jnp.concatenate([jnp.concatenate([s0_ref[d, 2 * p], z], axis=1),
                                               jnp.concatenate([z, s0_ref[d, 2 * p + 1]], axis=1)], axis=0)
            else:
                s_ref[d, p] = jnp.zeros((LANE, LANE), F32)

    def step(c, carry):
        rf = pl.ds(pl.multiple_of(c * C, C), C)
        rb = pl.ds(pl.multiple_of((nc - 1 - c) * C, C), C)
        jobs = []
        for p in range(pairs):
            cols = slice(p * LANE, (p + 1) * LANE)
            for d, (rows, wl_ref, al_ref) in enumerate(((rf, wlf_ref, alf_ref), (rb, wlb_ref, alb_ref))):
                jobs.append((r_ref[rows, cols], k_ref[rows, cols], v_ref[rows, cols], wl_ref[rows, cols],
                             al_ref[rows, cols], kk_ref[:, cols], ka_ref[:, cols], s_ref[d, p], d == 1))
        results = _interleave([chunk(*job) for job in jobs])
        for n, (y, s_new) in enumerate(results):
            p, d = divmod(n, 2)
            cols = slice(p * LANE, (p + 1) * LANE)
            if d == 0:
                yf_ref[rf, cols] = y
            else:
                yb_ref[rb, cols] = y
            s_ref[d, p] = s_new
        return carry

    lax.fori_loop(0, nc, step, 0)
    for d in range(2):
        for p in range(pairs):
            s = s_ref[d, p]
            sfin_ref[d, 2 * p] = s[:HD, :HD]
            sfin_ref[d, 2 * p + 1] = s[HD:, HD:]


def _rwkv_scan(r, k, v, wl_f, wl_b, al_f, al_b, k_k, k_a, s0, row0, nseq, L, dst=None):
    pairs = 4
    width = pairs * LANE
    rb0 = row0 // L
    blk = pl.BlockSpec((L, width), lambda b, hp: (rb0 + b, hp))
    vec = pl.BlockSpec((1, width), lambda b, hp: (0, hp))
    st = pl.BlockSpec((None, 2, 2 * pairs, RWKV_HEAD_DIM, RWKV_HEAD_DIM), lambda b, hp: (b, 0, hp, 0, 0))
    has_init = s0 is not None
    in_specs = [blk] * 7 + [vec, vec] + ([st] if has_init else [])
    args = [r, k, v, wl_f, wl_b, al_f, al_b, k_k.reshape(1, D_MODEL), k_a.reshape(1, D_MODEL)]
    if has_init:
        args.append(s0)
    aliases = {}
    if dst is not None:
        aliases = {len(args): 0, len(args) + 1: 1}
        in_specs += [pl.BlockSpec(memory_space=pl.ANY)] * 2
        args += list(dst)
    yshape = jax.ShapeDtypeStruct((T_ALL, D_MODEL), F32)
    return pl.pallas_call(
        functools.partial(_rwkv_scan_body, L=L, has_init=has_init, has_dst=dst is not None, pairs=pairs),
        grid=(nseq, RWKV_HEADS // (2 * pairs)),
        in_specs=in_specs,
        out_specs=[blk] * 2 + [st],
        out_shape=[yshape, yshape,
                   jax.ShapeDtypeStruct((nseq, 2, RWKV_HEADS, RWKV_HEAD_DIM, RWKV_HEAD_DIM), F32)],
        input_output_aliases=aliases,
        scratch_shapes=[pltpu.VMEM((2, pairs, LANE, LANE), F32)],
        compiler_params=_params(2, 2 * 9 * L * width * 4 + (8 << 20)),
        name="rwkv_scan",
    )(*args)


def _rwkv_post_body(yf_ref, yb_ref, r_ref, k_ref, v_ref, alf_ref, alb_ref, g_ref,
                    ka_ref, rk_ref, lnw_ref, lnb_ref, o_ref, *, tc):
    HD = RWKV_HEAD_DIM
    same_head = ((lax.broadcasted_iota(jnp.int32, (LANE, LANE), 0) // HD)
                 == (lax.broadcasted_iota(jnp.int32, (LANE, LANE), 1) // HD))
    ones_bd = same_head.astype(BF16)

    def head_sum(x):
        hi = x.astype(BF16)
        lo = (x - hi.astype(F32)).astype(BF16)
        return (jnp.dot(hi, ones_bd, preferred_element_type=F32) + jnp.dot(lo, ones_bd, preferred_element_type=F32))

    for c in range(tc // LANE):
        sl = slice(c * LANE, (c + 1) * LANE)
        y = yf_ref[:, sl] + yb_ref[:, sl]
        mean = head_sum(y) * (1.0 / HD)
        yc = y - mean
        var = head_sum(yc * yc) * (1.0 / HD)
        yn = yc * lax.rsqrt(var + RWKV_GN_EPS) * lnw_ref[:, sl] + lnb_ref[:, sl]
        k = k_ref[:, sl]
        ka = ka_ref[:, sl]
        kd_sum = (k * (1.0 + (jax.nn.sigmoid(alf_ref[:, sl]) - 1.0) * ka)
                  + k * (1.0 + (jax.nn.sigmoid(alb_ref[:, sl]) - 1.0) * ka))
        bonus = head_sum(r_ref[:, sl] * kd_sum * rk_ref[:, sl]) * v_ref[:, sl]
        o_ref[:, sl] = ((yn + bonus) * g_ref[:, sl]).astype(o_ref.dtype)


def _rwkv_post(yf, yb, r, k, v, al_f, al_b, g, k_a, r_k, ln_w, ln_b):
    tm, tc = 512, 512
    blk = pl.BlockSpec((tm, tc), lambda i, j: (i, j))
    vec = pl.BlockSpec((1, tc), lambda i, j: (0, j))
    row = lambda x: x.reshape(1, D_MODEL)
    return pl.pallas_call(
        functools.partial(_rwkv_post_body, tc=tc),
        grid=(T_ALL // tm, D_MODEL // tc),
        in_specs=[blk] * 8 + [vec] * 4,
        out_specs=blk,
        out_shape=jax.ShapeDtypeStruct((T_ALL, D_MODEL), BF16),
        compiler_params=_params(2, 32 << 20),
        name="rwkv_post",
    )(yf, yb, r, k, v, al_f, al_b, g, row(k_a), row(r_k), row(ln_w), row(ln_b))


def _pad_lora(w_down, w_up):
    inner = w_down.shape[-1]
    pad = (-inner) % LANE
    return (jnp.pad(w_down, ((0, 0), (0, pad))), jnp.pad(w_up, ((0, pad), (0, 0))))


def kernel(x_prompt, x_sample, cache_attn_k, cache_attn_v, state_rwkv, c, c_ctx, ada_w, ada_b, norm1_g, norm2_g, mlp_w1, mlp_w2, attn_wqkv, attn_wo, attn_sink, hy_w_in, hy_conv_w, hy_conv_b, hy_f_w1, hy_f_b1, hy_f_w2, hy_f_b2, hy_f_w3, hy_f_b3, hy_f_freq, hy_f_wout, hy_bias, hy_w_out, rw_mu, rw_wr, rw_wk, rw_wv, rw_wo, rw_w0, rw_w1, rw_w2, rw_a0, rw_a1, rw_a2, rw_g1, rw_g2, rw_k_k, rw_k_a, rw_r_k, rw_ln_w, rw_ln_b, final_norm_g):
    D = D_MODEL
    x = jnp.concatenate([x_prompt.reshape(T_P, D), x_sample.reshape(T_S, D)], axis=0)
    cond = jnp.concatenate([c_ctx[None, :], c, jnp.zeros((N_COND - 1 - DEC_BATCH, D), F32)], axis=0)
    mod = _ada_table(cond, ada_w, ada_b).reshape(DEPTH, N_COND, 6, 1, D)
    kv_w = ATTN_KV_HEADS * ATTN_HEAD_DIM
    ck_all = cache_attn_k.reshape(DEC_BATCH, -1, PAST_LEN, kv_w)
    cv_all = cache_attn_v.reshape(DEC_BATCH, -1, PAST_LEN, kv_w)
    dense = dict(tm=1024, tn=512)
    new_k, new_v, new_s = [], [], []
    for i in range(DEPTH):
        kind, j = i % N_MIXERS, i // N_MIXERS
        sh1, sc1, gt1, sh2, sc2, gt2 = (mod[i, :, n] for n in range(6))
        if kind == 0:
            h = _norm_mod(x, norm1_g[i], sh1, sc1, BF16)
            qkv = _matmul(h, attn_wqkv[j], out_dtype=F32, **dense)
            qd = ATTN_HEADS * ATTN_HEAD_DIM
            new_k.append(qkv[:T_P, qd:qd + kv_w].reshape(BATCH, SEQ, ATTN_KV_HEADS, ATTN_HEAD_DIM))
            new_v.append(qkv[:T_P, qd + kv_w:].reshape(BATCH, SEQ, ATTN_KV_HEADS, ATTN_HEAD_DIM))
            o = _attn_latent(qkv, ck_all, cv_all, j, attn_sink[j], _attn_context(qkv, attn_sink[j]))
            x = _matmul(o, attn_wo[j], res=x, gate=gt1, **dense)
        elif kind == 1:
            h = _norm_mod(x, norm1_g[i], sh1, sc1, BF16)
            u = _matmul(h, hy_w_in[j], out_dtype=F32, **dense)
            filt = (hy_f_w1[j], hy_f_b1[j], hy_f_w2[j], hy_f_b2[j], hy_f_w3[j], hy_f_b3[j],
                    hy_f_freq[j], hy_f_wout[j])
            o = _hyena_mix(u, 0, BATCH, SEQ, hy_conv_w[j], hy_conv_b[j], _hyena_spectrum(SEQ, filt), hy_bias[j])
            o = _hyena_mix(u, T_P, DEC_BATCH, DEC_SEQ, hy_conv_w[j], hy_conv_b[j],
                           _hyena_spectrum(DEC_SEQ, filt), hy_bias[j], dst=o)
            x = _matmul(o, hy_w_out[j], res=x, gate=gt1, **dense)
        else:
            h = _norm_mod(x, norm1_g[i], sh1, sc1, F32)
            xr, xw, xk, xv, xa, xg = _rwkv_token_mix(h, rw_mu[j])
            r = _matmul(xr, rw_wr[j], **dense)
            k = _matmul(xk, rw_wk[j], **dense)
            v = _matmul(xv, rw_wv[j], **dense)
            g = _matmul(_matmul(xg, rw_g1[j], tm=1024, tn=256, out_dtype=BF16, act="sigmoid"), rw_g2[j], **dense)
            wl, al = [], []
            for d in range(2):
                w1p, w2p = _pad_lora(rw_w1[j, d], rw_w2[j, d])
                a1p, a2p = _pad_lora(rw_a1[j, d], rw_a2[j, d])
                wl.append(_matmul(_matmul(xw, w1p, tm=1024, tn=LANE, out_dtype=BF16, act="tanh"), w2p,
                                  bias=rw_w0[j, d], **dense))
                al.append(_matmul(_matmul(xa, a1p, tm=1024, tn=LANE, out_dtype=BF16), a2p,
                                  bias=rw_a0[j, d], **dense))
            scan = (r, k, v, wl[0], wl[1], al[0], al[1], rw_k_k[j], rw_k_a[j])
            yf, yb, s_fin = _rwkv_scan(*scan, None, 0, BATCH, SEQ)
            yf, yb, _ = _rwkv_scan(*scan, state_rwkv[:, j], T_P, DEC_BATCH, DEC_SEQ, dst=(yf, yb))
            new_s.append(s_fin)
            o = _rwkv_post(yf, yb, r, k, v, al[0], al[1], g, rw_k_a[j], rw_r_k[j], rw_ln_w[j], rw_ln_b[j])
            x = _matmul(o, rw_wo[j], res=x, gate=gt1, **dense)
        h = _norm_mod(x, norm2_g[i], sh2, sc2, BF16)
        hid = _matmul(h, mlp_w1[i], out_dtype=BF16, act="relu2", name="mlp_up", **dense)
        x = _matmul(hid, mlp_w2[i].astype(BF16), tm=1024, tn=1024, tk=2048, res=x, gate=gt2, name="mlp_down")
    y = _norm_mod(x, final_norm_g, mod[0, :, 0], mod[0, :, 0], F32, modulate=False)
    return (y[:T_P].reshape(BATCH, SEQ, D), y[T_P:].reshape(DEC_BATCH, DEC_SEQ, D),
            jnp.stack(new_k, axis=1), jnp.stack(new_v, axis=1), jnp.stack(new_s, axis=1))
```

```python
import functools
import math

import numpy as np
import jax
import jax.numpy as jnp
from jax import lax
from jax.experimental import pallas as pl
from jax.experimental.pallas import tpu as pltpu

F32 = jnp.float32
BF16 = jnp.bfloat16

D_MODEL = 2048
BATCH = 16
SEQ = 256
DEPTH = 4
DEC_BATCH = 8
DEC_SEQ = 1024
PAST_LEN = 512
GRID_W = 64
N_MIXERS = 3

ATTN_HEADS = 16
ATTN_KV_HEADS = 4
ATTN_HEAD_DIM = 128
ATTN_GROUP = ATTN_HEADS // ATTN_KV_HEADS
WINDOW = 128
ATTN_BLOCK = 128
ROPE_BASE = 10000.0
NEG_INF = -1e30

HYENA_EMB = 33
HYENA_TARGET = 1e-2
HYENA_FAST_DECAY = 0.3
HYENA_SLOW_DECAY = 1.5

RWKV_HEAD_DIM = 64
RWKV_HEADS = D_MODEL // RWKV_HEAD_DIM
RWKV_GN_EPS = 64e-5
RWKV_CHUNK = 64
NORM_EPS = 1e-6
D_FF = 4 * D_MODEL

T_P = BATCH * SEQ
T_S = DEC_BATCH * DEC_SEQ
T_ALL = T_P + T_S
N_COND = 16

LANE = 128
V7X_VMEM_LIMIT = 56 * 1024 * 1024


def _params(n_axes, vmem_bytes):
    return pltpu.CompilerParams(dimension_semantics=("arbitrary",) * n_axes,
                                vmem_limit_bytes=int(min(max(vmem_bytes, 16 * 2**20), V7X_VMEM_LIMIT)))


def _cond_row(i, tm):
    n_p = T_P // tm
    per = DEC_SEQ // tm
    return jnp.where(i < n_p, 0, 1 + (i - n_p) // per)


def _dot_nt(a, b, **kw):
    return lax.dot_general(a, b, (((1,), (1,)), ((), ())), preferred_element_type=F32, **kw)


def _mm_body(*refs, nk, tm, split_a, cast_w, act, has_bias, has_res, has_norm, modulate):
    it = iter(refs)
    a_ref = next(it)
    a2_ref = next(it) if split_a else None
    w_ref = next(it)
    bias_ref = next(it) if has_bias else None
    res_ref = next(it) if has_res else None
    gate_ref = next(it) if has_res else None
    g_ref = next(it) if has_norm else None
    sh_ref = next(it) if has_norm and modulate else None
    sc_ref = next(it) if has_norm and modulate else None
    o_ref = next(it)
    h_ref = next(it) if has_norm else None
    acc_ref = next(it) if nk > 1 else None
    wb_ref = next(it) if cast_w else None

    if cast_w:
        @pl.when(pl.program_id(1) == 0)
        def _():
            wb_ref[...] = w_ref[...].astype(BF16)
        w = wb_ref[...]
    else:
        w = w_ref[...].astype(BF16)
    a = a_ref[...].astype(BF16)
    if split_a:
        a = jnp.where(pl.program_id(1) < T_P // tm, a, a2_ref[...].astype(BF16))
    p = jnp.dot(a, w, preferred_element_type=F32)

    def finish(acc):
        if has_bias:
            acc = acc + bias_ref[...]
        if act == "relu2":
            acc = jnp.square(jnp.maximum(acc, 0.0))
        elif act == "tanh":
            acc = jnp.tanh(acc)
        elif act == "sigmoid":
            acc = jax.nn.sigmoid(acc)
        if has_res:
            acc = res_ref[...] + gate_ref[...] * acc
        o_ref[...] = acc.astype(o_ref.dtype)
        if has_norm:
            y = acc * lax.rsqrt(jnp.mean(acc * acc, axis=-1, keepdims=True) + NORM_EPS) * g_ref[...]
            if modulate:
                y = y * (1.0 + sc_ref[...]) + sh_ref[...]
            h_ref[...] = y.astype(h_ref.dtype)

    if nk == 1:
        finish(p)
    else:
        k = pl.program_id(2)

        @pl.when(k == 0)
        def _():
            acc_ref[...] = p

        @pl.when(k > 0)
        def _():
            acc_ref[...] += p

        @pl.when(k == nk - 1)
        def _():
            finish(acc_ref[...])


def _matmul(a, w, *, tm, tn, tk=None, layer=None, out_dtype=F32, act=None, bias=None, res=None, gate=None,
            norm=None, name="matmul"):
    split_a = isinstance(a, (tuple, list))
    a_list = list(a) if split_a else [a]
    M = sum(x.shape[0] for x in a_list)
    K = a_list[0].shape[1]
    N = w.shape[-1]
    tk = K if tk is None else tk
    nk = K // tk
    assert M % tm == 0 and N % tn == 0 and K % tk == 0
    cast_w = nk == 1 and w.dtype != BF16
    has_bias = bias is not None
    has_res = res is not None
    has_norm = norm is not None
    modulate = has_norm and norm[1] is not None
    n_p = T_P // tm
    if split_a:
        assert M == T_ALL and a_list[0].shape[0] == T_P
        in_specs = [pl.BlockSpec((tm, tk), lambda j, i, k: (jnp.minimum(i, n_p - 1), k)),
                    pl.BlockSpec((tm, tk), lambda j, i, k: (jnp.maximum(i - n_p, 0), k))]
    else:
        in_specs = [pl.BlockSpec((tm, tk), lambda j, i, k: (i, k))]
    once = dict(pipeline_mode=pl.Buffered(1)) if (N == tn and nk == 1) else {}
    if layer is None:
        in_specs.append(pl.BlockSpec((tk, tn), lambda j, i, k: (k, j), **once))
    else:
        in_specs.append(pl.BlockSpec((None, tk, tn), lambda j, i, k: (layer, k, j), **once))
    args = a_list + [w]
    if has_bias:
        in_specs.append(pl.BlockSpec((1, tn), lambda j, i, k: (0, j)))
        args.append(bias.reshape(1, N))
    cond_spec = pl.BlockSpec((None, 1, tn), lambda j, i, k: (_cond_row(i, tm), 0, j))
    if has_res:
        in_specs += [pl.BlockSpec((tm, tn), lambda j, i, k: (i, j)), cond_spec]
        args += [res, gate]
    out_specs = pl.BlockSpec((tm, tn), lambda j, i, k: (i, j))
    out_shape = jax.ShapeDtypeStruct((M, N), out_dtype)
    h_bytes = 0
    if has_norm:
        assert tn == N
        g, shift, scale, h_dtype = norm
        in_specs.append(pl.BlockSpec((1, tn), lambda j, i, k: (0, 0)))
        args.append(g.reshape(1, N))
        if modulate:
            in_specs += [cond_spec, cond_spec]
            args += [shift, scale]
        out_specs = [out_specs, pl.BlockSpec((tm, tn), lambda j, i, k: (i, j))]
        out_shape = [out_shape, jax.ShapeDtypeStruct((M, N), h_dtype)]
        h_bytes = tm * tn * jnp.dtype(h_dtype).itemsize
    scratch = []
    if nk > 1:
        scratch.append(pltpu.VMEM((tm, tn), F32))
    if cast_w:
        scratch.append(pltpu.VMEM((tk, tn), BF16))
    w_bufs = 1 if once else 2
    vmem = (2 * (len(a_list) * tm * tk * a_list[0].dtype.itemsize + tm * tn * jnp.dtype(out_dtype).itemsize + h_bytes
                 + (tm * tn * 4 if has_res else 0)) + w_bufs * tk * tn * w.dtype.itemsize
            + 3 * tm * tn * 4 + (tk * tn * 2 if cast_w else 0) + tm * tk * 2 + (4 << 20))
    return pl.pallas_call(
        functools.partial(_mm_body, nk=nk, tm=tm, split_a=split_a, cast_w=cast_w, act=act, has_bias=has_bias,
                          has_res=has_res, has_norm=has_norm, modulate=modulate),
        grid=(N // tn, M // tm, nk),
        in_specs=in_specs,
        out_specs=out_specs,
        out_shape=out_shape,
        scratch_shapes=scratch,
        compiler_params=_params(3, vmem),
        name=name,
    )(*args)


def _ada_body(c_ref, w_ref, b_ref, o_ref):
    c = c_ref[...]
    s = (c * jax.nn.sigmoid(c)).astype(BF16)
    o_ref[...] = jnp.dot(s, w_ref[...].astype(BF16), preferred_element_type=F32) + b_ref[...]


def _ada_table(cond, ada_w, ada_b):
    tn = 1024
    n6 = 6 * D_MODEL
    return pl.pallas_call(
        _ada_body,
        grid=(DEPTH, n6 // tn),
        in_specs=[pl.BlockSpec((N_COND, D_MODEL), lambda l, j: (0, 0)),
                  pl.BlockSpec((None, D_MODEL, tn), lambda l, j: (l, 0, j)),
                  pl.BlockSpec((None, 1, tn), lambda l, j: (l, 0, j))],
        out_specs=pl.BlockSpec((None, N_COND, tn), lambda l, j: (l, 0, j)),
        out_shape=jax.ShapeDtypeStruct((DEPTH, N_COND, n6), F32),
        compiler_params=_params(2, 2 * D_MODEL * tn * 4 + D_MODEL * tn * 2 + (4 << 20)),
        name="ada_table",
    )(cond, ada_w, ada_b.reshape(DEPTH, 1, n6))


def _norm_body(x_ref, g_ref, sh_ref, sc_ref, o_ref, *, modulate):
    x = x_ref[...]
    y = x * lax.rsqrt(jnp.mean(x * x, axis=-1, keepdims=True) + NORM_EPS) * g_ref[...]
    if modulate:
        y = y * (1.0 + sc_ref[...]) + sh_ref[...]
    o_ref[...] = y.astype(o_ref.dtype)


def _norm_mod(x, g, shift, scale, out_dtype, modulate=True, row0=0, nrows=T_ALL):
    tm = 512
    t0 = row0 // tm
    cond_spec = pl.BlockSpec((None, 1, D_MODEL), lambda i: (_cond_row(t0 + i, tm), 0, 0))
    return pl.pallas_call(
        functools.partial(_norm_body, modulate=modulate),
        grid=(nrows // tm,),
        in_specs=[pl.BlockSpec((tm, D_MODEL), lambda i: (t0 + i, 0)),
                  pl.BlockSpec((1, D_MODEL), lambda i: (0, 0)),
                  cond_spec, cond_spec],
        out_specs=pl.BlockSpec((tm, D_MODEL), lambda i: (i, 0)),
        out_shape=jax.ShapeDtypeStruct((nrows, D_MODEL), out_dtype),
        compiler_params=_params(1, 6 * tm * D_MODEL * 4 + (4 << 20)),
        name="norm_mod",
    )(x, g.reshape(1, D_MODEL), shift, scale)


def _softmax_sink_pv(parts, sink_col):
    m = sink_col
    for s, _ in parts:
        m = jnp.maximum(m, jnp.max(s, axis=-1, keepdims=True))
    denom = jnp.exp(sink_col - m)
    out = None
    for s, v in parts:
        p = jnp.exp(s - m)
        denom = denom + jnp.sum(p, axis=-1, keepdims=True)
        pv = jnp.dot(p.astype(BF16), v, preferred_element_type=F32)
        out = pv if out is None else out + pv
    return out / denom


def _attn_ctx_body(sink_ref, q_ref, k_ref, v_ref, o_ref):
    h = pl.program_id(1)
    scale = ATTN_HEAD_DIM ** -0.5
    k = k_ref[...].astype(BF16)
    v = v_ref[...].astype(BF16)
    for g in range(ATTN_GROUP):
        q = q_ref[:, g * ATTN_HEAD_DIM:(g + 1) * ATTN_HEAD_DIM].astype(BF16)
        s = _dot_nt(q, k) * scale
        sink = jnp.full((SEQ, 1), sink_ref[h * ATTN_GROUP + g], F32)
        o = _softmax_sink_pv([(s, v)], sink)
        o_ref[:, g * ATTN_HEAD_DIM:(g + 1) * ATTN_HEAD_DIM] = o.astype(o_ref.dtype)


def _attn_context(qkv, sink):
    gw = ATTN_GROUP * ATTN_HEAD_DIM
    kcol = ATTN_HEADS * ATTN_HEAD_DIM // ATTN_HEAD_DIM
    vcol = kcol + ATTN_KV_HEADS
    return pl.pallas_call(
        _attn_ctx_body,
        grid=(BATCH, ATTN_KV_HEADS),
        in_specs=[pl.BlockSpec(memory_space=pltpu.SMEM),
                  pl.BlockSpec((SEQ, gw), lambda b, h: (b, h)),
                  pl.BlockSpec((SEQ, ATTN_HEAD_DIM), lambda b, h: (b, kcol + h)),
                  pl.BlockSpec((SEQ, ATTN_HEAD_DIM), lambda b, h: (b, vcol + h))],
        out_specs=pl.BlockSpec((SEQ, gw), lambda b, h: (b, h)),
        out_shape=jax.ShapeDtypeStruct((T_P, ATTN_HEADS * ATTN_HEAD_DIM), BF16),
        compiler_params=_params(2, 16 << 20),
        name="attn_context",
    )(sink, qkv, qkv, qkv)


def _rope(x, cos, sin_lo, sin_hi):
    return (x * cos + pltpu.roll(x, ATTN_HEAD_DIM - 32, axis=1) * sin_lo
            + pltpu.roll(x, 32, axis=1) * sin_hi)


def _attn_lat_body(sink_ref, q_ref, k_ref, v_ref, ck_ref, cv_ref, cos_ref, slo_ref, shi_ref, o_ref, kr_ref):
    h = pl.program_id(1)
    scale = ATTN_HEAD_DIM ** -0.5
    blk = ATTN_BLOCK
    nb = DEC_SEQ // blk
    rows = ATTN_GROUP * blk
    kr_ref[...] = _rope(k_ref[...], cos_ref[...], slo_ref[...], shi_ref[...]).astype(BF16)
    ck = ck_ref[...].astype(BF16)
    cv = cv_ref[...].astype(BF16)
    row_head = lax.broadcasted_iota(jnp.int32, (rows, 1), 0) // blk
    sink = jnp.zeros((rows, 1), F32)
    for g in range(ATTN_GROUP):
        sink = jnp.where(row_head == g, sink_ref[h * ATTN_GROUP + g], sink)
    q_in_blk = lax.broadcasted_iota(jnp.int32, (rows, 3 * blk), 0) % blk
    k_in_band = lax.broadcasted_iota(jnp.int32, (rows, 3 * blk), 1)

    def block(j, carry):
        r0 = pl.multiple_of(j * blk, blk)
        k0 = pl.multiple_of(jnp.clip(j - 1, 0, nb - 3) * blk, blk)
        cos = cos_ref[pl.ds(r0, blk), :]
        slo = slo_ref[pl.ds(r0, blk), :]
        shi = shi_ref[pl.ds(r0, blk), :]
        qs = jnp.concatenate(
            [_rope(q_ref[pl.ds(r0, blk), g * ATTN_HEAD_DIM:(g + 1) * ATTN_HEAD_DIM], cos, slo, shi)
             for g in range(ATTN_GROUP)], axis=0).astype(BF16)
        kl = kr_ref[pl.ds(k0, 3 * blk), :]
        vl = v_ref[pl.ds(k0, 3 * blk), :].astype(BF16)
        rel = (k0 + k_in_band) - (r0 + q_in_blk)
        s_loc = jnp.where(jnp.abs(rel) <= WINDOW, _dot_nt(qs, kl) * scale, NEG_INF)
        s_ctx = _dot_nt(qs, ck) * scale
        o = _softmax_sink_pv([(s_loc, vl), (s_ctx, cv)], sink)
        for g in range(ATTN_GROUP):
            o_ref[pl.ds(r0, blk), g * ATTN_HEAD_DIM:(g + 1) * ATTN_HEAD_DIM] = (
                o[g * blk:(g + 1) * blk].astype(o_ref.dtype))
        return carry

    lax.fori_loop(0, nb, block, 0)


def _rope_tables():
    t = np.arange(DEC_SEQ)
    half = ATTN_HEAD_DIM // 2
    inv = ROPE_BASE ** (-np.arange(0, half, 2, dtype=np.float64) / half)
    ang_r = (t // GRID_W).astype(np.float64)[:, None] * inv
    ang_c = (t % GRID_W).astype(np.float64)[:, None] * inv
    ang = np.concatenate([ang_r, ang_r, ang_c, ang_c], axis=1)
    lane = np.arange(ATTN_HEAD_DIM)
    first = (lane % 64) < 32
    cos = np.cos(ang)
    sin = np.sin(ang)
    sin_lo = np.where(first, -sin, 0.0)
    sin_hi = np.where(first, 0.0, sin)
    return (jnp.asarray(cos, F32), jnp.asarray(sin_lo, F32), jnp.asarray(sin_hi, F32))


def _attn_latent(qkv, cache_k, cache_v, layer, sink):
    gw = ATTN_GROUP * ATTN_HEAD_DIM
    kcol = ATTN_HEADS
    vcol = kcol + ATTN_KV_HEADS
    rb0 = T_P // DEC_SEQ
    cos, slo, shi = _rope_tables()
    tab = pl.BlockSpec((DEC_SEQ, ATTN_HEAD_DIM), lambda b, h: (0, 0))
    cache_spec = pl.BlockSpec((None, None, PAST_LEN, ATTN_HEAD_DIM), lambda b, h: (b, layer, 0, h))
    return pl.pallas_call(
        _attn_lat_body,
        grid=(DEC_BATCH, ATTN_KV_HEADS),
        in_specs=[pl.BlockSpec(memory_space=pltpu.SMEM),
                  pl.BlockSpec((DEC_SEQ, gw), lambda b, h: (rb0 + b, h)),
                  pl.BlockSpec((DEC_SEQ, ATTN_HEAD_DIM), lambda b, h: (rb0 + b, kcol + h)),
                  pl.BlockSpec((DEC_SEQ, ATTN_HEAD_DIM), lambda b, h: (rb0 + b, vcol + h)),
                  cache_spec, cache_spec, tab, tab, tab],
        out_specs=pl.BlockSpec((DEC_SEQ, gw), lambda b, h: (b, h)),
        out_shape=jax.ShapeDtypeStruct((T_S, ATTN_HEADS * ATTN_HEAD_DIM), BF16),
        scratch_shapes=[pltpu.VMEM((DEC_SEQ, ATTN_HEAD_DIM), BF16)],
        compiler_params=_params(2, 32 << 20),
        name="attn_latent",
    )(sink, qkv, qkv, qkv, cache_k, cache_v, cos, slo, shi)


def _dft_matrices(L):
    N = 2 * L
    p = np.arange(N)[:, None]
    n = np.arange(N)[None, :]
    f = np.where(p < L, p, p - L)
    ang = 2.0 * np.pi * ((f * n) % N) / N
    fwd = np.where(p < L, np.cos(ang), np.where(p == L, np.cos(np.pi * n), -np.sin(ang)))
    t = np.arange(L)[:, None]
    pp = np.arange(N)[None, :]
    ff = np.where(pp < L, pp, pp - L)
    ang2 = 2.0 * np.pi * ((ff * t) % N) / N
    inv = np.where(pp < L, np.where(pp == 0, 1.0, 2.0) * np.cos(ang2),
                   np.where(pp == L, np.cos(np.pi * t), -2.0 * np.sin(ang2))) / N
    return fwd, inv


def _hyena_filter_time(L, f_w1, f_b1, f_w2, f_b2, f_w3, f_b3, freq, f_wout):
    t = jnp.linspace(0.0, 1.0, L, dtype=F32)[:, None]
    bands = (HYENA_EMB - 1) // 2
    w = 2 * math.pi * jnp.arange(L, dtype=F32)[:, None] / L
    fr = jnp.linspace(1e-4, bands - 1, bands, dtype=F32)[None, :]
    z = jnp.concatenate([t, jnp.cos(fr * w), -jnp.sin(fr * w)], axis=-1)
    hdn = jnp.sin(freq * (z @ f_w1 + f_b1))
    hdn = jnp.sin(freq * (hdn @ f_w2 + f_b2))
    hdn = jnp.sin(freq * (hdn @ f_w3 + f_b3))
    filt = (hdn @ f_wout).reshape(L, 2, D_MODEL)
    deltas = jnp.linspace(math.log(HYENA_TARGET) / HYENA_SLOW_DECAY,
                          math.log(HYENA_TARGET) / HYENA_FAST_DECAY, D_MODEL, dtype=F32)
    filt = filt * jnp.exp(-t * jnp.abs(deltas))[:, None, :]
    return filt.reshape(2 * L, D_MODEL)


def _hyena_body(x0_ref, x1_ref, v_ref, cw0_ref, cw1_ref, cwv_ref, cb0_ref, cb1_ref, cbv_ref,
                hf_ref, fb_ref, fwd_ref, inv_ref, o_ref, *, L):
    row = lax.broadcasted_iota(jnp.int32, (L, 1), 0)

    def conv3(u_ref, w_ref, b_ref):
        u = u_ref[...]
        prev = jnp.where(row == 0, 0.0, pltpu.roll(u, 1, axis=0))
        nxt = jnp.where(row == L - 1, 0.0, pltpu.roll(u, L - 1, axis=0))
        return prev * w_ref[0:1, :] + u * w_ref[1:2, :] + nxt * w_ref[2:3, :] + b_ref[...]

    x0 = conv3(x0_ref, cw0_ref, cb0_ref)
    x1 = conv3(x1_ref, cw1_ref, cb1_ref)
    v = conv3(v_ref, cwv_ref, cbv_ref)
    z = v * x1
    zf = jnp.dot(fwd_ref[...], z.astype(BF16), preferred_element_type=F32)
    zt, zb = zf[:L], zf[L:]
    ht, hb = hf_ref[0:L, :], hf_ref[L:2 * L, :]
    first = row == 0
    yt = zt * ht - jnp.where(first, 0.0, zb * hb)
    yb = jnp.where(first, zb * hb, zt * hb + zb * ht)
    yf = jnp.concatenate([yt, yb], axis=0).astype(BF16)
    y = jnp.dot(inv_ref[...], yf, preferred_element_type=F32) + z * fb_ref[...]
    o_ref[...] = (y * x0).astype(o_ref.dtype)


def _hyena_mix(u, row0, nseq, L, conv_w, conv_b, hf, fbias):
    tc = 256
    nc = D_MODEL // tc
    fwd, inv = _dft_matrices(L)
    fwd = jnp.asarray(fwd[:, :L], F32).astype(BF16)
    inv = jnp.asarray(inv, F32).astype(BF16)
    rb0 = row0 // L

    def uspec(part):
        return pl.BlockSpec((L, tc), lambda b, j: (rb0 + b, part * nc + j))

    def wspec(part):
        return pl.BlockSpec((3, tc), lambda b, j: (0, part * nc + j))

    def bspec(part):
        return pl.BlockSpec((1, tc), lambda b, j: (0, part * nc + j))

    cb = conv_b.reshape(1, 3 * D_MODEL)
    in_specs = [uspec(0), uspec(1), uspec(2), wspec(0), wspec(1), wspec(2), bspec(0), bspec(1), bspec(2),
                pl.BlockSpec((2 * L, tc), lambda b, j: (0, j)),
                pl.BlockSpec((1, tc), lambda b, j: (0, j)),
                pl.BlockSpec((2 * L, L), lambda b, j: (0, 0)),
                pl.BlockSpec((L, 2 * L), lambda b, j: (0, 0))]
    args = [u, u, u, conv_w, conv_w, conv_w, cb, cb, cb, hf, fbias.reshape(1, D_MODEL), fwd, inv]
    return pl.pallas_call(
        functools.partial(_hyena_body, L=L),
        grid=(nseq, nc),
        in_specs=in_specs,
        out_specs=pl.BlockSpec((L, tc), lambda b, j: (b, j)),
        out_shape=jax.ShapeDtypeStruct((nseq * L, D_MODEL), BF16),
        compiler_params=_params(2, 40 << 20),
        name="hyena_mix",
    )(*args)


def _hyena_spectrum(L, filt_params):
    filt = _hyena_filter_time(L, *filt_params)
    fwd, _ = _dft_matrices(L)
    n = 2 * L
    t = np.arange(L)
    comb = np.zeros((n, n))
    comb[:, 2 * t] = fwd[:, t]
    comb[:, 2 * t[1:] + 1] = fwd[:, n - t[1:]]
    return _matmul(jnp.asarray(comb, F32), filt, tm=min(n, 1024), tn=512, out_dtype=F32, name="hyena_spectrum")


def _rwkv_mix_body(h_ref, hp_ref, hn_ref, mu_ref, *o_refs, tm):
    i = pl.program_id(0)
    n_p = T_P // tm
    per = DEC_SEQ // tm
    pos = (i - n_p) % per
    first = jnp.logical_or(i < n_p, pos == 0)
    last = jnp.logical_or(i < n_p, pos == per - 1)
    h = h_ref[...]
    row = lax.broadcasted_iota(jnp.int32, (tm, 1), 0)
    before = jnp.where(first, 0.0, hp_ref[7:8, :])
    after = jnp.where(last, 0.0, hn_ref[0:1, :])
    prev = jnp.where(row == 0, before, pltpu.roll(h, 1, axis=0))
    nxt = jnp.where(row == tm - 1, after, pltpu.roll(h, tm - 1, axis=0))
    xx = 0.5 * (prev + nxt) - h
    for n, o_ref in enumerate(o_refs):
        o_ref[...] = (h + xx * mu_ref[n:n + 1, :]).astype(o_ref.dtype)


def _rwkv_token_mix(h, mu):
    tm = SEQ
    tc = 512
    hb = tm // 8
    nrb = T_ALL // 8
    out = jax.ShapeDtypeStruct((T_ALL, D_MODEL), BF16)
    return pl.pallas_call(
        functools.partial(_rwkv_mix_body, tm=tm),
        grid=(T_ALL // tm, D_MODEL // tc),
        in_specs=[pl.BlockSpec((tm, tc), lambda i, j: (i, j)),
                  pl.BlockSpec((8, tc), lambda i, j: (jnp.maximum(i * hb - 1, 0), j)),
                  pl.BlockSpec((8, tc), lambda i, j: (jnp.minimum((i + 1) * hb, nrb - 1), j)),
                  pl.BlockSpec((6, tc), lambda i, j: (0, j))],
        out_specs=[pl.BlockSpec((tm, tc), lambda i, j: (i, j))] * 6,
        out_shape=[out] * 6,
        compiler_params=_params(2, 16 << 20),
        name="rwkv_token_mix",
    )(h, h, h, mu)


def _bdot(a, b):
    return jnp.dot(a.astype(BF16), b.astype(BF16), preferred_element_type=F32)


TRI_BASE = 8


def _tri_inverse_steps(n):
    m = n.shape[0]
    ri = lax.broadcasted_iota(jnp.int32, n.shape, 0)
    rj = lax.broadcasted_iota(jnp.int32, n.shape, 1)
    eye = (ri == rj).astype(F32)
    nb = jnp.where(ri // TRI_BASE == rj // TRI_BASE, n, 0.0)
    p = eye - nb
    q = _bdot(nb, nb)
    yield
    qp = _bdot(jnp.concatenate([q, p], axis=0), q)
    q, p = qp[:m], p + qp[m:]
    yield
    t = p + _bdot(p, q)
    yield
    s = TRI_BASE
    while s < RWKV_CHUNK:
        off = jnp.where((ri // (2 * s) == rj // (2 * s)) & (ri // s != rj // s), n, 0.0)
        mt = _bdot(off, t)
        yield
        t = t - _bdot(t, mt)
        yield
        s *= 2
    return t


def _interleave(gens):
    results = [None] * len(gens)
    live = list(enumerate(gens))
    while live:
        still = []
        for n, g in live:
            try:
                next(g)
                still.append((n, g))
            except StopIteration as done:
                results[n] = done.value
        live = still
    return results


def _rwkv_scan_body(*refs, L, has_init, pairs):
    it = iter(refs)
    r_ref, k_ref, v_ref, wlf_ref, wlb_ref, alf_ref, alb_ref, kk_ref, ka_ref = (next(it) for _ in range(9))
    s0_ref = next(it) if has_init else None
    yf_ref, yb_ref, sfin_ref, s_ref = next(it), next(it), next(it), next(it)
    C = RWKV_CHUNK
    HD = RWKV_HEAD_DIM
    nc = L // C
    lane = lax.broadcasted_iota(jnp.int32, (1, LANE), 1)
    head0 = lane < HD
    ones_bd = ((lax.broadcasted_iota(jnp.int32, (LANE, LANE), 0) // HD)
               == (lax.broadcasted_iota(jnp.int32, (LANE, LANE), 1) // HD)).astype(BF16)
    ci = lax.broadcasted_iota(jnp.int32, (C, C), 0)
    cj = lax.broadcasted_iota(jnp.int32, (C, C), 1)
    bi = lax.broadcasted_iota(jnp.int32, (2 * C, 2 * C), 0) % C
    bj = lax.broadcasted_iota(jnp.int32, (2 * C, 2 * C), 1) % C

    def stack(x):
        return jnp.concatenate([jnp.where(head0, x, 0.0), jnp.where(head0, 0.0, x)], axis=0)

    def chunk(r, k, v, wl, al, kk_w, ka_w, s, reverse):
        sp = jnp.maximum(-wl, 0.0) + jnp.log(1.0 + jnp.exp(-jnp.abs(wl)))
        lw = -jnp.exp(-sp - 0.5)
        a = jax.nn.sigmoid(al)
        kkr = k * kk_w
        kk = kkr * lax.rsqrt(_bdot(kkr * kkr, ones_bd) + 1e-12)
        kd = k * (1.0 + (a - 1.0) * ka_w)
        b = kk * a
        tri = ((cj >= ci) if reverse else (cj <= ci)).astype(BF16)
        lw_hi = lw.astype(BF16)
        lw_lo = (lw - lw_hi.astype(F32)).astype(BF16)
        g_inc = (jnp.dot(tri, lw_hi, preferred_element_type=F32)
                 + jnp.dot(tri, lw_lo, preferred_element_type=F32))
        yield
        g_end = g_inc[0:1, :] if reverse else g_inc[C - 1:C, :]
        e_neg = jnp.exp(-g_inc)
        e_end = jnp.exp(g_end - g_inc)
        lk = stack(kk * jnp.exp(g_inc - lw)).astype(BF16)
        lr = stack(r * jnp.exp(g_inc)).astype(BF16)
        bs = stack(b * e_neg).astype(BF16)
        ks = stack(kd * e_neg).astype(BF16)
        vs = stack(v)
        vs_b = vs.astype(BF16)
        big = _dot_nt(jnp.concatenate([lk, lr], axis=0), jnp.concatenate([bs, ks], axis=0))
        yield
        strict = (bj > bi) if reverse else (bj < bi)
        incl = (bj >= bi) if reverse else (bj <= bi)
        n_bd = jnp.where(strict, big[:2 * C, :2 * C], 0.0)
        m_dk = jnp.where(strict, big[:2 * C, 2 * C:], 0.0)
        a_all = jnp.where(jnp.concatenate([incl, incl], axis=1), big[2 * C:, :], 0.0)
        s_b = s.astype(BF16)
        w = _dot_nt(lk, s_b) + _bdot(m_dk, vs_b)
        ys0 = _dot_nt(lr, s_b)
        t_inv = yield from _tri_inverse_steps(n_bd)
        us = -_bdot(t_inv, w)
        yield
        uv = jnp.concatenate([us, vs], axis=0)
        ys = ys0 + _bdot(a_all, uv)
        s_new = s * jnp.exp(g_end) + _bdot(uv.T, jnp.concatenate([stack(b * e_end), stack(kd * e_end)], axis=0))
        return ys[:C] + ys[C:], s_new

    z = jnp.zeros((HD, HD), F32)
    for d in range(2):
        for p in range(pairs):
            if has_init:
                s_ref[d, p] = jnp.concatenate([jnp.concatenate([s0_ref[d, 2 * p], z], axis=1),
                                               jnp.concatenate([z, s0_ref[d, 2 * p + 1]], axis=1)], axis=0)
            else:
                s_ref[d, p] = jnp.zeros((LANE, LANE), F32)

    def step(c, carry):
        rf = pl.ds(pl.multiple_of(c * C, C), C)
        rb = pl.ds(pl.multiple_of((nc - 1 - c) * C, C), C)
        jobs = []
        for p in range(pairs):
            cols = slice(p * LANE, (p + 1) * LANE)
            for d, (rows, wl_ref, al_ref) in enumerate(((rf, wlf_ref, alf_ref), (rb, wlb_ref, alb_ref))):
                jobs.append((r_ref[rows, cols], k_ref[rows, cols], v_ref[rows, cols], wl_ref[rows, cols],
                             al_ref[rows, cols], kk_ref[:, cols], ka_ref[:, cols], s_ref[d, p], d == 1))
        results = _interleave([chunk(*job) for job in jobs])
        for n, (y, s_new) in enumerate(results):
            p, d = divmod(n, 2)
            cols = slice(p * LANE, (p + 1) * LANE)
            if d == 0:
                yf_ref[rf, cols] = y
            else:
                yb_ref[rb, cols] = y
            s_ref[d, p] = s_new
        return carry

    lax.fori_loop(0, nc, step, 0)
    for d in range(2):
        for p in range(pairs):
            s = s_ref[d, p]
            sfin_ref[d, 2 * p] = s[:HD, :HD]
            sfin_ref[d, 2 * p + 1] = s[HD:, HD:]


def _rwkv_scan(r, k, v, wl_f, wl_b, al_f, al_b, k_k, k_a, s0, row0, nseq, L):
    pairs = 4
    width = pairs * LANE
    rb0 = row0 // L
    blk = pl.BlockSpec((L, width), lambda b, hp: (rb0 + b, hp))
    vec = pl.BlockSpec((1, width), lambda b, hp: (0, hp))
    st = pl.BlockSpec((None, 2, 2 * pairs, RWKV_HEAD_DIM, RWKV_HEAD_DIM), lambda b, hp: (b, 0, hp, 0, 0))
    has_init = s0 is not None
    in_specs = [blk] * 7 + [vec, vec] + ([st] if has_init else [])
    args = [r, k, v, wl_f, wl_b, al_f, al_b, k_k.reshape(1, D_MODEL), k_a.reshape(1, D_MODEL)]
    if has_init:
        args.append(s0)
    yshape = jax.ShapeDtypeStruct((nseq * L, D_MODEL), F32)
    return pl.pallas_call(
        functools.partial(_rwkv_scan_body, L=L, has_init=has_init, pairs=pairs),
        grid=(nseq, RWKV_HEADS // (2 * pairs)),
        in_specs=in_specs,
        out_specs=[pl.BlockSpec((L, width), lambda b, hp: (b, hp))] * 2 + [st],
        out_shape=[yshape, yshape,
                   jax.ShapeDtypeStruct((nseq, 2, RWKV_HEADS, RWKV_HEAD_DIM, RWKV_HEAD_DIM), F32)],
        scratch_shapes=[pltpu.VMEM((2, pairs, LANE, LANE), F32)],
        compiler_params=_params(2, 2 * 9 * L * width * 4 + (8 << 20)),
        name="rwkv_scan",
    )(*args)


def _rwkv_post_body(yfp_ref, ybp_ref, yfs_ref, ybs_ref, r_ref, k_ref, v_ref, alf_ref, alb_ref, g_ref,
                    ka_ref, rk_ref, lnw_ref, lnb_ref, o_ref, *, tm, tc):
    HD = RWKV_HEAD_DIM
    is_prompt = pl.program_id(0) < T_P // tm
    same_head = ((lax.broadcasted_iota(jnp.int32, (LANE, LANE), 0) // HD)
                 == (lax.broadcasted_iota(jnp.int32, (LANE, LANE), 1) // HD))
    ones_bd = same_head.astype(BF16)

    def head_sum(x):
        hi = x.astype(BF16)
        lo = (x - hi.astype(F32)).astype(BF16)
        return (jnp.dot(hi, ones_bd, preferred_element_type=F32) + jnp.dot(lo, ones_bd, preferred_element_type=F32))

    for c in range(tc // LANE):
        sl = slice(c * LANE, (c + 1) * LANE)
        y = jnp.where(is_prompt, yfp_ref[:, sl] + ybp_ref[:, sl], yfs_ref[:, sl] + ybs_ref[:, sl])
        mean = head_sum(y) * (1.0 / HD)
        yc = y - mean
        var = head_sum(yc * yc) * (1.0 / HD)
        yn = yc * lax.rsqrt(var + RWKV_GN_EPS) * lnw_ref[:, sl] + lnb_ref[:, sl]
        k = k_ref[:, sl]
        ka = ka_ref[:, sl]
        kd_sum = (k * (1.0 + (jax.nn.sigmoid(alf_ref[:, sl]) - 1.0) * ka)
                  + k * (1.0 + (jax.nn.sigmoid(alb_ref[:, sl]) - 1.0) * ka))
        bonus = head_sum(r_ref[:, sl] * kd_sum * rk_ref[:, sl]) * v_ref[:, sl]
        o_ref[:, sl] = ((yn + bonus) * g_ref[:, sl]).astype(o_ref.dtype)


def _rwkv_post(y_prompt, y_latent, r, k, v, al_f, al_b, g, k_a, r_k, ln_w, ln_b):
    tm, tc = 512, 512
    n_p = T_P // tm
    blk = pl.BlockSpec((tm, tc), lambda i, j: (i, j))
    blk_p = pl.BlockSpec((tm, tc), lambda i, j: (jnp.minimum(i, n_p - 1), j))
    blk_s = pl.BlockSpec((tm, tc), lambda i, j: (jnp.maximum(i - n_p, 0), j))
    vec = pl.BlockSpec((1, tc), lambda i, j: (0, j))
    row = lambda x: x.reshape(1, D_MODEL)
    return pl.pallas_call(
        functools.partial(_rwkv_post_body, tm=tm, tc=tc),
        grid=(T_ALL // tm, D_MODEL // tc),
        in_specs=[blk_p, blk_p, blk_s, blk_s] + [blk] * 6 + [vec] * 4,
        out_specs=blk,
        out_shape=jax.ShapeDtypeStruct((T_ALL, D_MODEL), BF16),
        compiler_params=_params(2, 40 << 20),
        name="rwkv_post",
    )(*y_prompt, *y_latent, r, k, v, al_f, al_b, g, row(k_a), row(r_k), row(ln_w), row(ln_b))


def _pad_lora(w_down, w_up):
    inner = w_down.shape[-1]
    pad = (-inner) % LANE
    return (jnp.pad(w_down, ((0, 0), (0, pad))), jnp.pad(w_up, ((0, pad), (0, 0))))


def kernel(x_prompt, x_sample, cache_attn_k, cache_attn_v, state_rwkv, c, c_ctx, ada_w, ada_b, norm1_g, norm2_g, mlp_w1, mlp_w2, attn_wqkv, attn_wo, attn_sink, hy_w_in, hy_conv_w, hy_conv_b, hy_f_w1, hy_f_b1, hy_f_w2, hy_f_b2, hy_f_w3, hy_f_b3, hy_f_freq, hy_f_wout, hy_bias, hy_w_out, rw_mu, rw_wr, rw_wk, rw_wv, rw_wo, rw_w0, rw_w1, rw_w2, rw_a0, rw_a1, rw_a2, rw_g1, rw_g2, rw_k_k, rw_k_a, rw_r_k, rw_ln_w, rw_ln_b, final_norm_g):
    D = D_MODEL
    x = jnp.concatenate([x_prompt.reshape(T_P, D), x_sample.reshape(T_S, D)], axis=0)
    cond = jnp.concatenate([c_ctx[None, :], c, jnp.zeros((N_COND - 1 - DEC_BATCH, D), F32)], axis=0)
    mod = _ada_table(cond, ada_w, ada_b).reshape(DEPTH, N_COND, 6, 1, D)
    kv_w = ATTN_KV_HEADS * ATTN_HEAD_DIM
    ck_all = cache_attn_k.reshape(DEC_BATCH, -1, PAST_LEN, kv_w)
    cv_all = cache_attn_v.reshape(DEC_BATCH, -1, PAST_LEN, kv_w)
    wide = dict(tm=1024, tn=1024)
    square = dict(tm=512, tn=D)
    bf = lambda w: w.astype(BF16)
    attn_wo_b, hy_w_out_b, mlp_w2_b = bf(attn_wo), bf(hy_w_out), bf(mlp_w2)
    rw_wr_b, rw_wk_b, rw_wv_b, rw_wo_b, rw_g2_b = bf(rw_wr), bf(rw_wk), bf(rw_wv), bf(rw_wo), bf(rw_g2)
    new_k, new_v, new_s = [], [], []
    for i in range(DEPTH):
        kind, j = i % N_MIXERS, i // N_MIXERS
        sh1, sc1, gt1, sh2, sc2, gt2 = (mod[i, :, n] for n in range(6))
        mix_out = dict(res=x, gate=gt1, norm=(norm2_g[i], sh2, sc2, BF16), **square)
        if kind == 0:
            h = _norm_mod(x, norm1_g[i], sh1, sc1, BF16)
            qkv = _matmul(h, attn_wqkv, layer=j, out_dtype=F32, **wide)
            qd = ATTN_HEADS * ATTN_HEAD_DIM
            new_k.append(qkv[:T_P, qd:qd + kv_w].reshape(BATCH, SEQ, ATTN_KV_HEADS, ATTN_HEAD_DIM))
            new_v.append(qkv[:T_P, qd + kv_w:].reshape(BATCH, SEQ, ATTN_KV_HEADS, ATTN_HEAD_DIM))
            o = (_attn_context(qkv, attn_sink[j]), _attn_latent(qkv, ck_all, cv_all, j, attn_sink[j]))
            x, h = _matmul(o, attn_wo_b, layer=j, **mix_out)
        elif kind == 1:
            h = _norm_mod(x, norm1_g[i], sh1, sc1, BF16)
            u = _matmul(h, hy_w_in, layer=j, out_dtype=F32, **wide)
            filt = (hy_f_w1[j], hy_f_b1[j], hy_f_w2[j], hy_f_b2[j], hy_f_w3[j], hy_f_b3[j],
                    hy_f_freq[j], hy_f_wout[j])
            o = (_hyena_mix(u, 0, BATCH, SEQ, hy_conv_w[j], hy_conv_b[j], _hyena_spectrum(SEQ, filt), hy_bias[j]),
                 _hyena_mix(u, T_P, DEC_BATCH, DEC_SEQ, hy_conv_w[j], hy_conv_b[j],
                            _hyena_spectrum(DEC_SEQ, filt), hy_bias[j]))
            x, h = _matmul(o, hy_w_out_b, layer=j, **mix_out)
        else:
            h = _norm_mod(x, norm1_g[i], sh1, sc1, F32)
            xr, xw, xk, xv, xa, xg = _rwkv_token_mix(h, rw_mu[j])
            r = _matmul(xr, rw_wr_b, layer=j, **square)
            k = _matmul(xk, rw_wk_b, layer=j, **square)
            v = _matmul(xv, rw_wv_b, layer=j, **square)
            g = _matmul(_matmul(xg, rw_g1, layer=j, tm=1024, tn=rw_g1.shape[-1], out_dtype=BF16, act="sigmoid"),
                        rw_g2_b, layer=j, **square)
            wl, al = [], []
            for d in range(2):
                w1p, w2p = _pad_lora(rw_w1[j, d], rw_w2[j, d])
                a1p, a2p = _pad_lora(rw_a1[j, d], rw_a2[j, d])
                wl.append(_matmul(_matmul(xw, w1p, tm=1024, tn=LANE, out_dtype=BF16, act="tanh"), w2p,
                                  bias=rw_w0[j, d], **square))
                al.append(_matmul(_matmul(xa, a1p, tm=1024, tn=LANE, out_dtype=BF16), a2p,
                                  bias=rw_a0[j, d], **square))
            scan = (r, k, v, wl[0], wl[1], al[0], al[1], rw_k_k[j], rw_k_a[j])
            yf_p, yb_p, s_fin = _rwkv_scan(*scan, None, 0, BATCH, SEQ)
            yf_s, yb_s, _ = _rwkv_scan(*scan, state_rwkv[:, j], T_P, DEC_BATCH, DEC_SEQ)
            new_s.append(s_fin)
            o = _rwkv_post((yf_p, yb_p), (yf_s, yb_s), r, k, v, al[0], al[1], g,
                           rw_k_a[j], rw_r_k[j], rw_ln_w[j], rw_ln_b[j])
            x, h = _matmul(o, rw_wo_b, layer=j, **mix_out)
        hid = _matmul(h, mlp_w1, layer=i, out_dtype=BF16, act="relu2", name="mlp_up", **wide)
        x = _matmul(hid, mlp_w2_b, layer=i, tm=1024, tn=1024, tk=2048, res=x, gate=gt2, name="mlp_down")
    final = lambda row0, nrows: _norm_mod(x, final_norm_g, mod[0, :, 0], mod[0, :, 0], F32, modulate=False,
                                          row0=row0, nrows=nrows)
    return (final(0, T_P).reshape(BATCH, SEQ, D), final(T_P, T_S).reshape(DEC_BATCH, DEC_SEQ, D),
            jnp.stack(new_k, axis=1), jnp.stack(new_v, axis=1), jnp.stack(new_s, axis=1))
```

```python
import functools
import math

import numpy as np
import jax
import jax.numpy as jnp
from jax import lax
from jax.experimental import pallas as pl
from jax.experimental.pallas import tpu as pltpu

F32 = jnp.float32
BF16 = jnp.bfloat16

D_MODEL = 2048
BATCH = 16
SEQ = 256
DEPTH = 4
DEC_BATCH = 8
DEC_SEQ = 1024
PAST_LEN = 512
GRID_W = 64
N_MIXERS = 3

ATTN_HEADS = 16
ATTN_KV_HEADS = 4
ATTN_HEAD_DIM = 128
ATTN_GROUP = ATTN_HEADS // ATTN_KV_HEADS
WINDOW = 128
ATTN_BLOCK = 128
ROPE_BASE = 10000.0
NEG_INF = -1e30

HYENA_EMB = 33
HYENA_TARGET = 1e-2
HYENA_FAST_DECAY = 0.3
HYENA_SLOW_DECAY = 1.5

RWKV_HEAD_DIM = 64
RWKV_HEADS = D_MODEL // RWKV_HEAD_DIM
RWKV_GN_EPS = 64e-5
RWKV_CHUNK = 64
NORM_EPS = 1e-6
D_FF = 4 * D_MODEL

T_P = BATCH * SEQ
T_S = DEC_BATCH * DEC_SEQ
T_ALL = T_P + T_S
N_COND = 16

LANE = 128
V7X_VMEM_LIMIT = 56 * 1024 * 1024


def _params(n_axes, vmem_bytes):
    return pltpu.CompilerParams(dimension_semantics=("arbitrary",) * n_axes,
                                vmem_limit_bytes=int(min(max(vmem_bytes, 16 * 2**20), V7X_VMEM_LIMIT)))


def _cond_row(i, tm):
    n_p = T_P // tm
    per = DEC_SEQ // tm
    return jnp.where(i < n_p, 0, 1 + (i - n_p) // per)


def _dot_nt(a, b, **kw):
    return lax.dot_general(a, b, (((1,), (1,)), ((), ())), preferred_element_type=F32, **kw)


def _mm_body(*refs, nk, tm, split_a, cast_w, act, has_bias, has_res, has_norm, modulate):
    it = iter(refs)
    a_ref = next(it)
    a2_ref = next(it) if split_a else None
    w_ref = next(it)
    bias_ref = next(it) if has_bias else None
    res_ref = next(it) if has_res else None
    gate_ref = next(it) if has_res else None
    g_ref = next(it) if has_norm else None
    sh_ref = next(it) if has_norm and modulate else None
    sc_ref = next(it) if has_norm and modulate else None
    o_ref = next(it)
    h_ref = next(it) if has_norm else None
    acc_ref = next(it) if nk > 1 else None
    wb_ref = next(it) if cast_w else None

    if cast_w:
        @pl.when(pl.program_id(1) == 0)
        def _():
            wb_ref[...] = w_ref[...].astype(BF16)
        w = wb_ref[...]
    else:
        w = w_ref[...].astype(BF16)
    a = a_ref[...].astype(BF16)
    if split_a:
        a = jnp.where(pl.program_id(1) < T_P // tm, a, a2_ref[...].astype(BF16))
    p = jnp.dot(a, w, preferred_element_type=F32)

    def finish(acc):
        if has_bias:
            acc = acc + bias_ref[...]
        if act == "relu2":
            acc = jnp.square(jnp.maximum(acc, 0.0))
        elif act == "tanh":
            acc = jnp.tanh(acc)
        elif act == "sigmoid":
            acc = jax.nn.sigmoid(acc)
        if has_res:
            acc = res_ref[...] + gate_ref[...] * acc
        o_ref[...] = acc.astype(o_ref.dtype)
        if has_norm:
            y = acc * lax.rsqrt(jnp.mean(acc * acc, axis=-1, keepdims=True) + NORM_EPS) * g_ref[...]
            if modulate:
                y = y * (1.0 + sc_ref[...]) + sh_ref[...]
            h_ref[...] = y.astype(h_ref.dtype)

    if nk == 1:
        finish(p)
    else:
        k = pl.program_id(2)

        @pl.when(k == 0)
        def _():
            acc_ref[...] = p

        @pl.when(k > 0)
        def _():
            acc_ref[...] += p

        @pl.when(k == nk - 1)
        def _():
            finish(acc_ref[...])


def _matmul(a, w, *, tm, tn, tk=None, layer=None, out_dtype=F32, act=None, bias=None, res=None, gate=None,
            norm=None, name="matmul"):
    split_a = isinstance(a, (tuple, list))
    a_list = list(a) if split_a else [a]
    M = sum(x.shape[0] for x in a_list)
    K = a_list[0].shape[1]
    N = w.shape[-1]
    tk = K if tk is None else tk
    nk = K // tk
    assert M % tm == 0 and N % tn == 0 and K % tk == 0
    cast_w = nk == 1 and w.dtype != BF16
    has_bias = bias is not None
    has_res = res is not None
    has_norm = norm is not None
    modulate = has_norm and norm[1] is not None
    n_p = T_P // tm
    if split_a:
        assert M == T_ALL and a_list[0].shape[0] == T_P
        in_specs = [pl.BlockSpec((tm, tk), lambda j, i, k: (jnp.minimum(i, n_p - 1), k)),
                    pl.BlockSpec((tm, tk), lambda j, i, k: (jnp.maximum(i - n_p, 0), k))]
    else:
        in_specs = [pl.BlockSpec((tm, tk), lambda j, i, k: (i, k))]
    once = dict(pipeline_mode=pl.Buffered(1)) if (N == tn and nk == 1) else {}
    if layer is None:
        in_specs.append(pl.BlockSpec((tk, tn), lambda j, i, k: (k, j), **once))
    else:
        in_specs.append(pl.BlockSpec((None, tk, tn), lambda j, i, k: (layer, k, j), **once))
    args = a_list + [w]
    if has_bias:
        in_specs.append(pl.BlockSpec((1, tn), lambda j, i, k: (0, j)))
        args.append(bias.reshape(1, N))
    cond_spec = pl.BlockSpec((None, 1, tn), lambda j, i, k: (_cond_row(i, tm), 0, j))
    if has_res:
        in_specs += [pl.BlockSpec((tm, tn), lambda j, i, k: (i, j)), cond_spec]
        args += [res, gate]
    out_specs = pl.BlockSpec((tm, tn), lambda j, i, k: (i, j))
    out_shape = jax.ShapeDtypeStruct((M, N), out_dtype)
    h_bytes = 0
    if has_norm:
        assert tn == N
        g, shift, scale, h_dtype = norm
        in_specs.append(pl.BlockSpec((1, tn), lambda j, i, k: (0, 0)))
        args.append(g.reshape(1, N))
        if modulate:
            in_specs += [cond_spec, cond_spec]
            args += [shift, scale]
        out_specs = [out_specs, pl.BlockSpec((tm, tn), lambda j, i, k: (i, j))]
        out_shape = [out_shape, jax.ShapeDtypeStruct((M, N), h_dtype)]
        h_bytes = tm * tn * jnp.dtype(h_dtype).itemsize
    scratch = []
    if nk > 1:
        scratch.append(pltpu.VMEM((tm, tn), F32))
    if cast_w:
        scratch.append(pltpu.VMEM((tk, tn), BF16))
    w_bufs = 1 if once else 2
    vmem = (2 * (len(a_list) * tm * tk * a_list[0].dtype.itemsize + tm * tn * jnp.dtype(out_dtype).itemsize + h_bytes
                 + (tm * tn * 4 if has_res else 0)) + w_bufs * tk * tn * w.dtype.itemsize
            + 3 * tm * tn * 4 + (tk * tn * 2 if cast_w else 0) + tm * tk * 2 + (4 << 20))
    return pl.pallas_call(
        functools.partial(_mm_body, nk=nk, tm=tm, split_a=split_a, cast_w=cast_w, act=act, has_bias=has_bias,
                          has_res=has_res, has_norm=has_norm, modulate=modulate),
        grid=(N // tn, M // tm, nk),
        in_specs=in_specs,
        out_specs=out_specs,
        out_shape=out_shape,
        scratch_shapes=scratch,
        compiler_params=_params(3, vmem),
        name=name,
    )(*args)


def _ada_body(c_ref, w_ref, b_ref, o_ref):
    c = c_ref[...]
    s = (c * jax.nn.sigmoid(c)).astype(BF16)
    o_ref[...] = jnp.dot(s, w_ref[...].astype(BF16), preferred_element_type=F32) + b_ref[...]


def _ada_table(cond, ada_w, ada_b):
    tn = 1024
    n6 = 6 * D_MODEL
    return pl.pallas_call(
        _ada_body,
        grid=(DEPTH, n6 // tn),
        in_specs=[pl.BlockSpec((N_COND, D_MODEL), lambda l, j: (0, 0)),
                  pl.BlockSpec((None, D_MODEL, tn), lambda l, j: (l, 0, j)),
                  pl.BlockSpec((None, 1, tn), lambda l, j: (l, 0, j))],
        out_specs=pl.BlockSpec((None, N_COND, tn), lambda l, j: (l, 0, j)),
        out_shape=jax.ShapeDtypeStruct((DEPTH, N_COND, n6), F32),
        compiler_params=_params(2, 2 * D_MODEL * tn * 4 + D_MODEL * tn * 2 + (4 << 20)),
        name="ada_table",
    )(cond, ada_w, ada_b.reshape(DEPTH, 1, n6))


def _norm_body(x_ref, g_ref, sh_ref, sc_ref, o_ref, *, modulate):
    x = x_ref[...]
    y = x * lax.rsqrt(jnp.mean(x * x, axis=-1, keepdims=True) + NORM_EPS) * g_ref[...]
    if modulate:
        y = y * (1.0 + sc_ref[...]) + sh_ref[...]
    o_ref[...] = y.astype(o_ref.dtype)


def _norm_mod(x, g, shift, scale, out_dtype, modulate=True, row0=0, nrows=T_ALL):
    tm = 512
    t0 = row0 // tm
    cond_spec = pl.BlockSpec((None, 1, D_MODEL), lambda i: (_cond_row(t0 + i, tm), 0, 0))
    return pl.pallas_call(
        functools.partial(_norm_body, modulate=modulate),
        grid=(nrows // tm,),
        in_specs=[pl.BlockSpec((tm, D_MODEL), lambda i: (t0 + i, 0)),
                  pl.BlockSpec((1, D_MODEL), lambda i: (0, 0)),
                  cond_spec, cond_spec],
        out_specs=pl.BlockSpec((tm, D_MODEL), lambda i: (i, 0)),
        out_shape=jax.ShapeDtypeStruct((nrows, D_MODEL), out_dtype),
        compiler_params=_params(1, 6 * tm * D_MODEL * 4 + (4 << 20)),
        name="norm_mod",
    )(x, g.reshape(1, D_MODEL), shift, scale)


def _softmax_sink_pv(parts, sink_col):
    m = sink_col
    for s, _ in parts:
        m = jnp.maximum(m, jnp.max(s, axis=-1, keepdims=True))
    denom = jnp.exp(sink_col - m)
    out = None
    for s, v in parts:
        p = jnp.exp(s - m)
        denom = denom + jnp.sum(p, axis=-1, keepdims=True)
        pv = jnp.dot(p.astype(BF16), v, preferred_element_type=F32)
        out = pv if out is None else out + pv
    return out / denom


def _attn_ctx_body(sink_ref, q_ref, k_ref, v_ref, o_ref):
    h = pl.program_id(1)
    scale = ATTN_HEAD_DIM ** -0.5
    k = k_ref[...].astype(BF16)
    v = v_ref[...].astype(BF16)
    for g in range(ATTN_GROUP):
        q = q_ref[:, g * ATTN_HEAD_DIM:(g + 1) * ATTN_HEAD_DIM].astype(BF16)
        s = _dot_nt(q, k) * scale
        sink = jnp.full((SEQ, 1), sink_ref[h * ATTN_GROUP + g], F32)
        o = _softmax_sink_pv([(s, v)], sink)
        o_ref[:, g * ATTN_HEAD_DIM:(g + 1) * ATTN_HEAD_DIM] = o.astype(o_ref.dtype)


def _attn_context(qkv, sink):
    gw = ATTN_GROUP * ATTN_HEAD_DIM
    kcol = ATTN_HEADS * ATTN_HEAD_DIM // ATTN_HEAD_DIM
    vcol = kcol + ATTN_KV_HEADS
    return pl.pallas_call(
        _attn_ctx_body,
        grid=(BATCH, ATTN_KV_HEADS),
        in_specs=[pl.BlockSpec(memory_space=pltpu.SMEM),
                  pl.BlockSpec((SEQ, gw), lambda b, h: (b, h)),
                  pl.BlockSpec((SEQ, ATTN_HEAD_DIM), lambda b, h: (b, kcol + h)),
                  pl.BlockSpec((SEQ, ATTN_HEAD_DIM), lambda b, h: (b, vcol + h))],
        out_specs=pl.BlockSpec((SEQ, gw), lambda b, h: (b, h)),
        out_shape=jax.ShapeDtypeStruct((T_P, ATTN_HEADS * ATTN_HEAD_DIM), BF16),
        compiler_params=_params(2, 16 << 20),
        name="attn_context",
    )(sink, qkv, qkv, qkv)


def _rope(x, cos, sin_lo, sin_hi):
    return (x * cos + pltpu.roll(x, ATTN_HEAD_DIM - 32, axis=1) * sin_lo
            + pltpu.roll(x, 32, axis=1) * sin_hi)


def _attn_lat_body(sink_ref, q_ref, k_ref, v_ref, ck_ref, cv_ref, cos_ref, slo_ref, shi_ref, o_ref, kr_ref):
    h = pl.program_id(1)
    scale = ATTN_HEAD_DIM ** -0.5
    blk = ATTN_BLOCK
    nb = DEC_SEQ // blk
    rows = ATTN_GROUP * blk
    kr_ref[...] = _rope(k_ref[...], cos_ref[...], slo_ref[...], shi_ref[...]).astype(BF16)
    ck = ck_ref[...].astype(BF16)
    cv = cv_ref[...].astype(BF16)
    row_head = lax.broadcasted_iota(jnp.int32, (rows, 1), 0) // blk
    sink = jnp.zeros((rows, 1), F32)
    for g in range(ATTN_GROUP):
        sink = jnp.where(row_head == g, sink_ref[h * ATTN_GROUP + g], sink)
    q_in_blk = lax.broadcasted_iota(jnp.int32, (rows, 3 * blk), 0) % blk
    k_in_band = lax.broadcasted_iota(jnp.int32, (rows, 3 * blk), 1)

    def block(j, carry):
        r0 = pl.multiple_of(j * blk, blk)
        k0 = pl.multiple_of(jnp.clip(j - 1, 0, nb - 3) * blk, blk)
        cos = cos_ref[pl.ds(r0, blk), :]
        slo = slo_ref[pl.ds(r0, blk), :]
        shi = shi_ref[pl.ds(r0, blk), :]
        qs = jnp.concatenate(
            [_rope(q_ref[pl.ds(r0, blk), g * ATTN_HEAD_DIM:(g + 1) * ATTN_HEAD_DIM], cos, slo, shi)
             for g in range(ATTN_GROUP)], axis=0).astype(BF16)
        kl = kr_ref[pl.ds(k0, 3 * blk), :]
        vl = v_ref[pl.ds(k0, 3 * blk), :].astype(BF16)
        rel = (k0 + k_in_band) - (r0 + q_in_blk)
        s_loc = jnp.where(jnp.abs(rel) <= WINDOW, _dot_nt(qs, kl) * scale, NEG_INF)
        s_ctx = _dot_nt(qs, ck) * scale
        o = _softmax_sink_pv([(s_loc, vl), (s_ctx, cv)], sink)
        for g in range(ATTN_GROUP):
            o_ref[pl.ds(r0, blk), g * ATTN_HEAD_DIM:(g + 1) * ATTN_HEAD_DIM] = (
                o[g * blk:(g + 1) * blk].astype(o_ref.dtype))
        return carry

    lax.fori_loop(0, nb, block, 0)


def _rope_tables():
    t = np.arange(DEC_SEQ)
    half = ATTN_HEAD_DIM // 2
    inv = ROPE_BASE ** (-np.arange(0, half, 2, dtype=np.float64) / half)
    ang_r = (t // GRID_W).astype(np.float64)[:, None] * inv
    ang_c = (t % GRID_W).astype(np.float64)[:, None] * inv
    ang = np.concatenate([ang_r, ang_r, ang_c, ang_c], axis=1)
    lane = np.arange(ATTN_HEAD_DIM)
    first = (lane % 64) < 32
    cos = np.cos(ang)
    sin = np.sin(ang)
    sin_lo = np.where(first, -sin, 0.0)
    sin_hi = np.where(first, 0.0, sin)
    return (jnp.asarray(cos, F32), jnp.asarray(sin_lo, F32), jnp.asarray(sin_hi, F32))


def _attn_latent(qkv, cache_k, cache_v, layer, sink):
    gw = ATTN_GROUP * ATTN_HEAD_DIM
    kcol = ATTN_HEADS
    vcol = kcol + ATTN_KV_HEADS
    rb0 = T_P // DEC_SEQ
    cos, slo, shi = _rope_tables()
    tab = pl.BlockSpec((DEC_SEQ, ATTN_HEAD_DIM), lambda b, h: (0, 0))
    cache_spec = pl.BlockSpec((None, None, PAST_LEN, ATTN_HEAD_DIM), lambda b, h: (b, layer, 0, h))
    return pl.pallas_call(
        _attn_lat_body,
        grid=(DEC_BATCH, ATTN_KV_HEADS),
        in_specs=[pl.BlockSpec(memory_space=pltpu.SMEM),
                  pl.BlockSpec((DEC_SEQ, gw), lambda b, h: (rb0 + b, h)),
                  pl.BlockSpec((DEC_SEQ, ATTN_HEAD_DIM), lambda b, h: (rb0 + b, kcol + h)),
                  pl.BlockSpec((DEC_SEQ, ATTN_HEAD_DIM), lambda b, h: (rb0 + b, vcol + h)),
                  cache_spec, cache_spec, tab, tab, tab],
        out_specs=pl.BlockSpec((DEC_SEQ, gw), lambda b, h: (b, h)),
        out_shape=jax.ShapeDtypeStruct((T_S, ATTN_HEADS * ATTN_HEAD_DIM), BF16),
        scratch_shapes=[pltpu.VMEM((DEC_SEQ, ATTN_HEAD_DIM), BF16)],
        compiler_params=_params(2, 32 << 20),
        name="attn_latent",
    )(sink, qkv, qkv, qkv, cache_k, cache_v, cos, slo, shi)


def _dft_matrices(L):
    N = 2 * L
    p = np.arange(N)[:, None]
    n = np.arange(N)[None, :]
    f = np.where(p < L, p, p - L)
    ang = 2.0 * np.pi * ((f * n) % N) / N
    fwd = np.where(p < L, np.cos(ang), np.where(p == L, np.cos(np.pi * n), -np.sin(ang)))
    t = np.arange(L)[:, None]
    pp = np.arange(N)[None, :]
    ff = np.where(pp < L, pp, pp - L)
    ang2 = 2.0 * np.pi * ((ff * t) % N) / N
    inv = np.where(pp < L, np.where(pp == 0, 1.0, 2.0) * np.cos(ang2),
                   np.where(pp == L, np.cos(np.pi * t), -2.0 * np.sin(ang2))) / N
    return fwd, inv


def _hyena_filter_time(L, f_w1, f_b1, f_w2, f_b2, f_w3, f_b3, freq, f_wout):
    t = jnp.linspace(0.0, 1.0, L, dtype=F32)[:, None]
    bands = (HYENA_EMB - 1) // 2
    w = 2 * math.pi * jnp.arange(L, dtype=F32)[:, None] / L
    fr = jnp.linspace(1e-4, bands - 1, bands, dtype=F32)[None, :]
    z = jnp.concatenate([t, jnp.cos(fr * w), -jnp.sin(fr * w)], axis=-1)
    hdn = jnp.sin(freq * (z @ f_w1 + f_b1))
    hdn = jnp.sin(freq * (hdn @ f_w2 + f_b2))
    hdn = jnp.sin(freq * (hdn @ f_w3 + f_b3))
    filt = (hdn @ f_wout).reshape(L, 2, D_MODEL)
    deltas = jnp.linspace(math.log(HYENA_TARGET) / HYENA_SLOW_DECAY,
                          math.log(HYENA_TARGET) / HYENA_FAST_DECAY, D_MODEL, dtype=F32)
    filt = filt * jnp.exp(-t * jnp.abs(deltas))[:, None, :]
    return filt.reshape(2 * L, D_MODEL)


def _hyena_body(x0_ref, x1_ref, v_ref, cw0_ref, cw1_ref, cwv_ref, cb0_ref, cb1_ref, cbv_ref,
                hf_ref, fb_ref, fwd_ref, inv_ref, o_ref, *, L):
    row = lax.broadcasted_iota(jnp.int32, (L, 1), 0)

    def conv3(u_ref, w_ref, b_ref):
        u = u_ref[...]
        prev = jnp.where(row == 0, 0.0, pltpu.roll(u, 1, axis=0))
        nxt = jnp.where(row == L - 1, 0.0, pltpu.roll(u, L - 1, axis=0))
        return prev * w_ref[0:1, :] + u * w_ref[1:2, :] + nxt * w_ref[2:3, :] + b_ref[...]

    x0 = conv3(x0_ref, cw0_ref, cb0_ref)
    x1 = conv3(x1_ref, cw1_ref, cb1_ref)
    v = conv3(v_ref, cwv_ref, cbv_ref)
    z = v * x1
    zf = jnp.dot(fwd_ref[...], z.astype(BF16), preferred_element_type=F32)
    zt, zb = zf[:L], zf[L:]
    ht, hb = hf_ref[0:L, :], hf_ref[L:2 * L, :]
    first = row == 0
    yt = zt * ht - jnp.where(first, 0.0, zb * hb)
    yb = jnp.where(first, zb * hb, zt * hb + zb * ht)
    yf = jnp.concatenate([yt, yb], axis=0).astype(BF16)
    y = jnp.dot(inv_ref[...], yf, preferred_element_type=F32) + z * fb_ref[...]
    o_ref[...] = (y * x0).astype(o_ref.dtype)


def _hyena_mix(u, row0, nseq, L, conv_w, conv_b, hf, fbias):
    tc = 256
    nc = D_MODEL // tc
    fwd, inv = _dft_matrices(L)
    fwd = jnp.asarray(fwd[:, :L], F32).astype(BF16)
    inv = jnp.asarray(inv, F32).astype(BF16)
    rb0 = row0 // L

    def uspec(part):
        return pl.BlockSpec((L, tc), lambda b, j: (rb0 + b, part * nc + j))

    def wspec(part):
        return pl.BlockSpec((3, tc), lambda b, j: (0, part * nc + j))

    def bspec(part):
        return pl.BlockSpec((1, tc), lambda b, j: (0, part * nc + j))

    cb = conv_b.reshape(1, 3 * D_MODEL)
    in_specs = [uspec(0), uspec(1), uspec(2), wspec(0), wspec(1), wspec(2), bspec(0), bspec(1), bspec(2),
                pl.BlockSpec((2 * L, tc), lambda b, j: (0, j)),
                pl.BlockSpec((1, tc), lambda b, j: (0, j)),
                pl.BlockSpec((2 * L, L), lambda b, j: (0, 0)),
                pl.BlockSpec((L, 2 * L), lambda b, j: (0, 0))]
    args = [u, u, u, conv_w, conv_w, conv_w, cb, cb, cb, hf, fbias.reshape(1, D_MODEL), fwd, inv]
    return pl.pallas_call(
        functools.partial(_hyena_body, L=L),
        grid=(nseq, nc),
        in_specs=in_specs,
        out_specs=pl.BlockSpec((L, tc), lambda b, j: (b, j)),
        out_shape=jax.ShapeDtypeStruct((nseq * L, D_MODEL), BF16),
        compiler_params=_params(2, 40 << 20),
        name="hyena_mix",
    )(*args)


def _hyena_spectrum(L, filt_params):
    filt = _hyena_filter_time(L, *filt_params)
    fwd, _ = _dft_matrices(L)
    n = 2 * L
    t = np.arange(L)
    comb = np.zeros((n, n))
    comb[:, 2 * t] = fwd[:, t]
    comb[:, 2 * t[1:] + 1] = fwd[:, n - t[1:]]
    return _matmul(jnp.asarray(comb, F32), filt, tm=min(n, 1024), tn=512, out_dtype=F32, name="hyena_spectrum")


def _rwkv_mix_body(x_ref, xp_ref, xn_ref, g_ref, sh_ref, sc_ref, mu_ref, *o_refs, tm):
    i = pl.program_id(0)
    n_p = T_P // tm
    per = DEC_SEQ // tm
    pos = (i - n_p) % per
    first = jnp.logical_or(i < n_p, pos == 0)
    last = jnp.logical_or(i < n_p, pos == per - 1)

    def norm(x):
        y = x * lax.rsqrt(jnp.mean(x * x, axis=-1, keepdims=True) + NORM_EPS) * g_ref[...]
        return y * (1.0 + sc_ref[...]) + sh_ref[...]

    h = norm(x_ref[...])
    row = lax.broadcasted_iota(jnp.int32, (tm, 1), 0)
    before = jnp.where(first, 0.0, norm(xp_ref[...])[7:8, :])
    after = jnp.where(last, 0.0, norm(xn_ref[...])[0:1, :])
    prev = jnp.where(row == 0, before, pltpu.roll(h, 1, axis=0))
    nxt = jnp.where(row == tm - 1, after, pltpu.roll(h, tm - 1, axis=0))
    xx = 0.5 * (prev + nxt) - h
    for n, o_ref in enumerate(o_refs):
        o_ref[...] = (h + xx * mu_ref[n:n + 1, :]).astype(o_ref.dtype)


def _rwkv_token_mix(x, g, shift, scale, mu):
    tm = SEQ
    hb = tm // 8
    nrb = T_ALL // 8
    out = jax.ShapeDtypeStruct((T_ALL, D_MODEL), BF16)
    cond_spec = pl.BlockSpec((None, 1, D_MODEL), lambda i: (_cond_row(i, tm), 0, 0))
    return pl.pallas_call(
        functools.partial(_rwkv_mix_body, tm=tm),
        grid=(T_ALL // tm,),
        in_specs=[pl.BlockSpec((tm, D_MODEL), lambda i: (i, 0)),
                  pl.BlockSpec((8, D_MODEL), lambda i: (jnp.maximum(i * hb - 1, 0), 0)),
                  pl.BlockSpec((8, D_MODEL), lambda i: (jnp.minimum((i + 1) * hb, nrb - 1), 0)),
                  pl.BlockSpec((1, D_MODEL), lambda i: (0, 0)),
                  cond_spec, cond_spec,
                  pl.BlockSpec((6, D_MODEL), lambda i: (0, 0))],
        out_specs=[pl.BlockSpec((tm, D_MODEL), lambda i: (i, 0))] * 6,
        out_shape=[out] * 6,
        compiler_params=_params(1, 32 << 20),
        name="rwkv_token_mix",
    )(x, x, x, g.reshape(1, D_MODEL), shift, scale, mu)


def _bdot(a, b):
    return jnp.dot(a.astype(BF16), b.astype(BF16), preferred_element_type=F32)


TRI_BASE = 8


def _tri_inverse_steps(n):
    m = n.shape[0]
    ri = lax.broadcasted_iota(jnp.int32, n.shape, 0)
    rj = lax.broadcasted_iota(jnp.int32, n.shape, 1)
    eye = (ri == rj).astype(F32)
    nb = jnp.where(ri // TRI_BASE == rj // TRI_BASE, n, 0.0)
    p = eye - nb
    q = _bdot(nb, nb)
    yield
    qp = _bdot(jnp.concatenate([q, p], axis=0), q)
    q, p = qp[:m], p + qp[m:]
    yield
    t = p + _bdot(p, q)
    yield
    s = TRI_BASE
    while s < RWKV_CHUNK:
        off = jnp.where((ri // (2 * s) == rj // (2 * s)) & (ri // s != rj // s), n, 0.0)
        mt = _bdot(off, t)
        yield
        t = t - _bdot(t, mt)
        yield
        s *= 2
    return t


def _interleave(gens):
    results = [None] * len(gens)
    live = list(enumerate(gens))
    while live:
        still = []
        for n, g in live:
            try:
                next(g)
                still.append((n, g))
            except StopIteration as done:
                results[n] = done.value
        live = still
    return results


def _rwkv_scan_body(*refs, L, has_init, pairs):
    it = iter(refs)
    r_ref, k_ref, v_ref, wlf_ref, wlb_ref, alf_ref, alb_ref, kk_ref, ka_ref = (next(it) for _ in range(9))
    s0_ref = next(it) if has_init else None
    yf_ref, yb_ref, sfin_ref, s_ref = next(it), next(it), next(it), next(it)
    C = RWKV_CHUNK
    HD = RWKV_HEAD_DIM
    nc = L // C
    lane = lax.broadcasted_iota(jnp.int32, (1, LANE), 1)
    head0 = lane < HD
    ones_bd = ((lax.broadcasted_iota(jnp.int32, (LANE, LANE), 0) // HD)
               == (lax.broadcasted_iota(jnp.int32, (LANE, LANE), 1) // HD)).astype(BF16)
    ci = lax.broadcasted_iota(jnp.int32, (C, C), 0)
    cj = lax.broadcasted_iota(jnp.int32, (C, C), 1)
    bi = lax.broadcasted_iota(jnp.int32, (2 * C, 2 * C), 0) % C
    bj = lax.broadcasted_iota(jnp.int32, (2 * C, 2 * C), 1) % C

    def stack(x):
        return jnp.concatenate([jnp.where(head0, x, 0.0), jnp.where(head0, 0.0, x)], axis=0)

    def chunk(r, k, v, wl, al, kk_w, ka_w, s, reverse):
        sp = jnp.maximum(-wl, 0.0) + jnp.log(1.0 + jnp.exp(-jnp.abs(wl)))
        lw = -jnp.exp(-sp - 0.5)
        a = jax.nn.sigmoid(al)
        kkr = k * kk_w
        kk = kkr * lax.rsqrt(_bdot(kkr * kkr, ones_bd) + 1e-12)
        kd = k * (1.0 + (a - 1.0) * ka_w)
        b = kk * a
        tri = ((cj >= ci) if reverse else (cj <= ci)).astype(BF16)
        lw_hi = lw.astype(BF16)
        lw_lo = (lw - lw_hi.astype(F32)).astype(BF16)
        g_inc = (jnp.dot(tri, lw_hi, preferred_element_type=F32)
                 + jnp.dot(tri, lw_lo, preferred_element_type=F32))
        yield
        g_end = g_inc[0:1, :] if reverse else g_inc[C - 1:C, :]
        e_neg = jnp.exp(-g_inc)
        e_end = jnp.exp(g_end - g_inc)
        lk = stack(kk * jnp.exp(g_inc - lw)).astype(BF16)
        lr = stack(r * jnp.exp(g_inc)).astype(BF16)
        bs = stack(b * e_neg).astype(BF16)
        ks = stack(kd * e_neg).astype(BF16)
        vs = stack(v)
        vs_b = vs.astype(BF16)
        big = _dot_nt(jnp.concatenate([lk, lr], axis=0), jnp.concatenate([bs, ks], axis=0))
        yield
        strict = (bj > bi) if reverse else (bj < bi)
        incl = (bj >= bi) if reverse else (bj <= bi)
        n_bd = jnp.where(strict, big[:2 * C, :2 * C], 0.0)
        m_dk = jnp.where(strict, big[:2 * C, 2 * C:], 0.0)
        a_all = jnp.where(jnp.concatenate([incl, incl], axis=1), big[2 * C:, :], 0.0)
        s_b = s.astype(BF16)
        w = _dot_nt(lk, s_b) + _bdot(m_dk, vs_b)
        ys0 = _dot_nt(lr, s_b)
        t_inv = yield from _tri_inverse_steps(n_bd)
        us = -_bdot(t_inv, w)
        yield
        uv = jnp.concatenate([us, vs], axis=0)
        ys = ys0 + _bdot(a_all, uv)
        s_new = s * jnp.exp(g_end) + _bdot(uv.T, jnp.concatenate([stack(b * e_end), stack(kd * e_end)], axis=0))
        return ys[:C] + ys[C:], s_new

    z = jnp.zeros((HD, HD), F32)
    for d in range(2):
        for p in range(pairs):
            if has_init:
                s_ref[d, p] = jnp.concatenate([jnp.concatenate([s0_ref[d, 2 * p], z], axis=1),
                                               jnp.concatenate([z, s0_ref[d, 2 * p + 1]], axis=1)], axis=0)
            else:
                s_ref[d, p] = jnp.zeros((LANE, LANE), F32)

    def step(c, carry):
        rf = pl.ds(pl.multiple_of(c * C, C), C)
        rb = pl.ds(pl.multiple_of((nc - 1 - c) * C, C), C)
        jobs = []
        for p in range(pairs):
            cols = slice(p * LANE, (p + 1) * LANE)
            for d, (rows, wl_ref, al_ref) in enumerate(((rf, wlf_ref, alf_ref), (rb, wlb_ref, alb_ref))):
                jobs.append((r_ref[rows, cols], k_ref[rows, cols], v_ref[rows, cols], wl_ref[rows, cols],
                             al_ref[rows, cols], kk_ref[:, cols], ka_ref[:, cols], s_ref[d, p], d == 1))
        results = _interleave([chunk(*job) for job in jobs])
        for n, (y, s_new) in enumerate(results):
            p, d = divmod(n, 2)
            cols = slice(p * LANE, (p + 1) * LANE)
            if d == 0:
                yf_ref[rf, cols] = y
            else:
                yb_ref[rb, cols] = y
            s_ref[d, p] = s_new
        return carry

    lax.fori_loop(0, nc, step, 0)
    for d in range(2):
        for p in range(pairs):
            s = s_ref[d, p]
            sfin_ref[d, 2 * p] = s[:HD, :HD]
            sfin_ref[d, 2 * p + 1] = s[HD:, HD:]


def _rwkv_scan(r, k, v, wl_f, wl_b, al_f, al_b, k_k, k_a, s0, row0, nseq, L):
    pairs = 4
    width = pairs * LANE
    rb0 = row0 // L
    blk = pl.BlockSpec((L, width), lambda b, hp: (rb0 + b, hp))
    vec = pl.BlockSpec((1, width), lambda b, hp: (0, hp))
    st = pl.BlockSpec((None, 2, 2 * pairs, RWKV_HEAD_DIM, RWKV_HEAD_DIM), lambda b, hp: (b, 0, hp, 0, 0))
    has_init = s0 is not None
    in_specs = [blk] * 7 + [vec, vec] + ([st] if has_init else [])
    args = [r, k, v, wl_f, wl_b, al_f, al_b, k_k.reshape(1, D_MODEL), k_a.reshape(1, D_MODEL)]
    if has_init:
        args.append(s0)
    yshape = jax.ShapeDtypeStruct((nseq * L, D_MODEL), F32)
    return pl.pallas_call(
        functools.partial(_rwkv_scan_body, L=L, has_init=has_init, pairs=pairs),
        grid=(nseq, RWKV_HEADS // (2 * pairs)),
        in_specs=in_specs,
        out_specs=[pl.BlockSpec((L, width), lambda b, hp: (b, hp))] * 2 + [st],
        out_shape=[yshape, yshape,
                   jax.ShapeDtypeStruct((nseq, 2, RWKV_HEADS, RWKV_HEAD_DIM, RWKV_HEAD_DIM), F32)],
        scratch_shapes=[pltpu.VMEM((2, pairs, LANE, LANE), F32)],
        compiler_params=_params(2, 2 * 9 * L * width * 4 + (8 << 20)),
        name="rwkv_scan",
    )(*args)


def _rwkv_post_body(yfp_ref, ybp_ref, yfs_ref, ybs_ref, r_ref, k_ref, v_ref, alf_ref, alb_ref, g_ref,
                    ka_ref, rk_ref, lnw_ref, lnb_ref, o_ref, *, tm, tc):
    HD = RWKV_HEAD_DIM
    is_prompt = pl.program_id(0) < T_P // tm
    same_head = ((lax.broadcasted_iota(jnp.int32, (LANE, LANE), 0) // HD)
                 == (lax.broadcasted_iota(jnp.int32, (LANE, LANE), 1) // HD))
    ones_bd = same_head.astype(BF16)

    def head_sum(x):
        hi = x.astype(BF16)
        lo = (x - hi.astype(F32)).astype(BF16)
        return (jnp.dot(hi, ones_bd, preferred_element_type=F32) + jnp.dot(lo, ones_bd, preferred_element_type=F32))

    for c in range(tc // LANE):
        sl = slice(c * LANE, (c + 1) * LANE)
        y = jnp.where(is_prompt, yfp_ref[:, sl] + ybp_ref[:, sl], yfs_ref[:, sl] + ybs_ref[:, sl])
        mean = head_sum(y) * (1.0 / HD)
        yc = y - mean
        var = head_sum(yc * yc) * (1.0 / HD)
        yn = yc * lax.rsqrt(var + RWKV_GN_EPS) * lnw_ref[:, sl] + lnb_ref[:, sl]
        k = k_ref[:, sl]
        ka = ka_ref[:, sl]
        kd_sum = (k * (1.0 + (jax.nn.sigmoid(alf_ref[:, sl]) - 1.0) * ka)
                  + k * (1.0 + (jax.nn.sigmoid(alb_ref[:, sl]) - 1.0) * ka))
        bonus = head_sum(r_ref[:, sl] * kd_sum * rk_ref[:, sl]) * v_ref[:, sl]
        o_ref[:, sl] = ((yn + bonus) * g_ref[:, sl]).astype(o_ref.dtype)


def _rwkv_post(y_prompt, y_latent, r, k, v, al_f, al_b, g, k_a, r_k, ln_w, ln_b):
    tm, tc = 512, 512
    n_p = T_P // tm
    blk = pl.BlockSpec((tm, tc), lambda i, j: (i, j))
    blk_p = pl.BlockSpec((tm, tc), lambda i, j: (jnp.minimum(i, n_p - 1), j))
    blk_s = pl.BlockSpec((tm, tc), lambda i, j: (jnp.maximum(i - n_p, 0), j))
    vec = pl.BlockSpec((1, tc), lambda i, j: (0, j))
    row = lambda x: x.reshape(1, D_MODEL)
    return pl.pallas_call(
        functools.partial(_rwkv_post_body, tm=tm, tc=tc),
        grid=(T_ALL // tm, D_MODEL // tc),
        in_specs=[blk_p, blk_p, blk_s, blk_s] + [blk] * 6 + [vec] * 4,
        out_specs=blk,
        out_shape=jax.ShapeDtypeStruct((T_ALL, D_MODEL), BF16),
        compiler_params=_params(2, 40 << 20),
        name="rwkv_post",
    )(*y_prompt, *y_latent, r, k, v, al_f, al_b, g, row(k_a), row(r_k), row(ln_w), row(ln_b))


def _pad_lora(w_down, w_up):
    inner = w_down.shape[-1]
    pad = (-inner) % LANE
    return (jnp.pad(w_down, ((0, 0), (0, pad))), jnp.pad(w_up, ((0, pad), (0, 0))))


def kernel(x_prompt, x_sample, cache_attn_k, cache_attn_v, state_rwkv, c, c_ctx, ada_w, ada_b, norm1_g, norm2_g, mlp_w1, mlp_w2, attn_wqkv, attn_wo, attn_sink, hy_w_in, hy_conv_w, hy_conv_b, hy_f_w1, hy_f_b1, hy_f_w2, hy_f_b2, hy_f_w3, hy_f_b3, hy_f_freq, hy_f_wout, hy_bias, hy_w_out, rw_mu, rw_wr, rw_wk, rw_wv, rw_wo, rw_w0, rw_w1, rw_w2, rw_a0, rw_a1, rw_a2, rw_g1, rw_g2, rw_k_k, rw_k_a, rw_r_k, rw_ln_w, rw_ln_b, final_norm_g):
    D = D_MODEL
    x = jnp.concatenate([x_prompt.reshape(T_P, D), x_sample.reshape(T_S, D)], axis=0)
    cond = jnp.concatenate([c_ctx[None, :], c, jnp.zeros((N_COND - 1 - DEC_BATCH, D), F32)], axis=0)
    mod = _ada_table(cond, ada_w, ada_b).reshape(DEPTH, N_COND, 6, 1, D)
    kv_w = ATTN_KV_HEADS * ATTN_HEAD_DIM
    ck_all = cache_attn_k.reshape(DEC_BATCH, -1, PAST_LEN, kv_w)
    cv_all = cache_attn_v.reshape(DEC_BATCH, -1, PAST_LEN, kv_w)
    wide = dict(tm=1024, tn=1024)
    square = dict(tm=512, tn=D)
    bf = lambda w: w.astype(BF16)
    attn_wo_b, hy_w_out_b, mlp_w2_b = bf(attn_wo), bf(hy_w_out), bf(mlp_w2)
    rw_wr_b, rw_wk_b, rw_wv_b, rw_wo_b, rw_g2_b = bf(rw_wr), bf(rw_wk), bf(rw_wv), bf(rw_wo), bf(rw_g2)
    new_k, new_v, new_s = [], [], []
    for i in range(DEPTH):
        kind, j = i % N_MIXERS, i // N_MIXERS
        sh1, sc1, gt1, sh2, sc2, gt2 = (mod[i, :, n] for n in range(6))
        mix_out = dict(res=x, gate=gt1, norm=(norm2_g[i], sh2, sc2, BF16), **square)
        if kind == 0:
            h = _norm_mod(x, norm1_g[i], sh1, sc1, BF16)
            qkv = _matmul(h, attn_wqkv, layer=j, out_dtype=F32, **wide)
            qd = ATTN_HEADS * ATTN_HEAD_DIM
            new_k.append(qkv[:T_P, qd:qd + kv_w].reshape(BATCH, SEQ, ATTN_KV_HEADS, ATTN_HEAD_DIM))
            new_v.append(qkv[:T_P, qd + kv_w:].reshape(BATCH, SEQ, ATTN_KV_HEADS, ATTN_HEAD_DIM))
            o = (_attn_context(qkv, attn_sink[j]), _attn_latent(qkv, ck_all, cv_all, j, attn_sink[j]))
            x, h = _matmul(o, attn_wo_b, layer=j, **mix_out)
        elif kind == 1:
            h = _norm_mod(x, norm1_g[i], sh1, sc1, BF16)
            u = _matmul(h, hy_w_in, layer=j, out_dtype=F32, **wide)
            filt = (hy_f_w1[j], hy_f_b1[j], hy_f_w2[j], hy_f_b2[j], hy_f_w3[j], hy_f_b3[j],
                    hy_f_freq[j], hy_f_wout[j])
            o = (_hyena_mix(u, 0, BATCH, SEQ, hy_conv_w[j], hy_conv_b[j], _hyena_spectrum(SEQ, filt), hy_bias[j]),
                 _hyena_mix(u, T_P, DEC_BATCH, DEC_SEQ, hy_conv_w[j], hy_conv_b[j],
                            _hyena_spectrum(DEC_SEQ, filt), hy_bias[j]))
            x, h = _matmul(o, hy_w_out_b, layer=j, **mix_out)
        else:
            xr, xw, xk, xv, xa, xg = _rwkv_token_mix(x, norm1_g[i], sh1, sc1, rw_mu[j])
            r = _matmul(xr, rw_wr_b, layer=j, **square)
            k = _matmul(xk, rw_wk_b, layer=j, **square)
            v = _matmul(xv, rw_wv_b, layer=j, **square)
            g = _matmul(_matmul(xg, rw_g1, layer=j, tm=1024, tn=rw_g1.shape[-1], out_dtype=BF16, act="sigmoid"),
                        rw_g2_b, layer=j, **square)
            wl, al = [], []
            for d in range(2):
                w1p, w2p = _pad_lora(rw_w1[j, d], rw_w2[j, d])
                a1p, a2p = _pad_lora(rw_a1[j, d], rw_a2[j, d])
                wl.append(_matmul(_matmul(xw, w1p, tm=1024, tn=LANE, out_dtype=BF16, act="tanh"), w2p,
                                  bias=rw_w0[j, d], **square))
                al.append(_matmul(_matmul(xa, a1p, tm=1024, tn=LANE, out_dtype=BF16), a2p,
                                  bias=rw_a0[j, d], **square))
            scan = (r, k, v, wl[0], wl[1], al[0], al[1], rw_k_k[j], rw_k_a[j])
            yf_p, yb_p, s_fin = _rwkv_scan(*scan, None, 0, BATCH, SEQ)
            yf_s, yb_s, _ = _rwkv_scan(*scan, state_rwkv[:, j], T_P, DEC_BATCH, DEC_SEQ)
            new_s.append(s_fin)
            o = _rwkv_post((yf_p, yb_p), (yf_s, yb_s), r, k, v, al[0], al[1], g,
                           rw_k_a[j], rw_r_k[j], rw_ln_w[j], rw_ln_b[j])
            x, h = _matmul(o, rw_wo_b, layer=j, **mix_out)
        hid = _matmul(h, mlp_w1, layer=i, out_dtype=BF16, act="relu2", name="mlp_up", **wide)
        x = _matmul(hid, mlp_w2_b, layer=i, tm=512, tn=512, res=x, gate=gt2, name="mlp_down")
    final = lambda row0, nrows: _norm_mod(x, final_norm_g, mod[0, :, 0], mod[0, :, 0], F32, modulate=False,
                                          row0=row0, nrows=nrows)
    return (final(0, T_P).reshape(BATCH, SEQ, D), final(T_P, T_S).reshape(DEC_BATCH, DEC_SEQ, D),
            jnp.stack(new_k, axis=1), jnp.stack(new_v, axis=1), jnp.stack(new_s, axis=1))
```

```python
import functools
import math

import numpy as np
import jax
import jax.numpy as jnp
from jax import lax
from jax.experimental import pallas as pl
from jax.experimental.pallas import tpu as pltpu

F32 = jnp.float32
BF16 = jnp.bfloat16

D_MODEL = 2048
BATCH = 16
SEQ = 256
DEPTH = 4
DEC_BATCH = 8
DEC_SEQ = 1024
PAST_LEN = 512
GRID_W = 64
N_MIXERS = 3

ATTN_HEADS = 16
ATTN_KV_HEADS = 4
ATTN_HEAD_DIM = 128
ATTN_GROUP = ATTN_HEADS // ATTN_KV_HEADS
WINDOW = 128
ATTN_BLOCK = 128
ROPE_BASE = 10000.0
NEG_INF = -1e30

HYENA_EMB = 33
HYENA_TARGET = 1e-2
HYENA_FAST_DECAY = 0.3
HYENA_SLOW_DECAY = 1.5

RWKV_HEAD_DIM = 64
RWKV_HEADS = D_MODEL // RWKV_HEAD_DIM
RWKV_GN_EPS = 64e-5
RWKV_CHUNK = 64
NORM_EPS = 1e-6
D_FF = 4 * D_MODEL

T_P = BATCH * SEQ
T_S = DEC_BATCH * DEC_SEQ
T_ALL = T_P + T_S
N_COND = 16

LANE = 128
V7X_VMEM_LIMIT = 56 * 1024 * 1024


def _params(n_axes, vmem_bytes):
    return pltpu.CompilerParams(dimension_semantics=("arbitrary",) * n_axes,
                                vmem_limit_bytes=int(min(max(vmem_bytes, 16 * 2**20), V7X_VMEM_LIMIT)))


def _cond_row(i, tm):
    n_p = T_P // tm
    per = DEC_SEQ // tm
    return jnp.where(i < n_p, 0, 1 + (i - n_p) // per)


def _dot_nt(a, b, **kw):
    return lax.dot_general(a, b, (((1,), (1,)), ((), ())), preferred_element_type=F32, **kw)


def _mm_body(*refs, nk, tm, split_a, cast_w, act, has_bias, has_res, has_norm, modulate):
    it = iter(refs)
    a_ref = next(it)
    a2_ref = next(it) if split_a else None
    w_ref = next(it)
    bias_ref = next(it) if has_bias else None
    res_ref = next(it) if has_res else None
    gate_ref = next(it) if has_res else None
    g_ref = next(it) if has_norm else None
    sh_ref = next(it) if has_norm and modulate else None
    sc_ref = next(it) if has_norm and modulate else None
    o_ref = next(it)
    h_ref = next(it) if has_norm else None
    acc_ref = next(it) if nk > 1 else None
    wb_ref = next(it) if cast_w else None

    if cast_w:
        @pl.when(pl.program_id(1) == 0)
        def _():
            wb_ref[...] = w_ref[...].astype(BF16)
        w = wb_ref[...]
    else:
        w = w_ref[...].astype(BF16)
    a = a_ref[...].astype(BF16)
    if split_a:
        a = jnp.where(pl.program_id(1) < T_P // tm, a, a2_ref[...].astype(BF16))
    p = jnp.dot(a, w, preferred_element_type=F32)

    def finish(acc):
        if has_bias:
            acc = acc + bias_ref[...]
        if act == "relu2":
            acc = jnp.square(jnp.maximum(acc, 0.0))
        elif act == "tanh":
            acc = jnp.tanh(acc)
        elif act == "sigmoid":
            acc = jax.nn.sigmoid(acc)
        if has_res:
            acc = res_ref[...] + gate_ref[...] * acc
        o_ref[...] = acc.astype(o_ref.dtype)
        if has_norm:
            y = acc * lax.rsqrt(jnp.mean(acc * acc, axis=-1, keepdims=True) + NORM_EPS) * g_ref[...]
            if modulate:
                y = y * (1.0 + sc_ref[...]) + sh_ref[...]
            h_ref[...] = y.astype(h_ref.dtype)

    if nk == 1:
        finish(p)
    else:
        k = pl.program_id(2)

        @pl.when(k == 0)
        def _():
            acc_ref[...] = p

        @pl.when(k > 0)
        def _():
            acc_ref[...] += p

        @pl.when(k == nk - 1)
        def _():
            finish(acc_ref[...])


def _matmul(a, w, *, tm, tn, tk=None, layer=None, out_dtype=F32, act=None, bias=None, res=None, gate=None,
            norm=None, name="matmul"):
    split_a = isinstance(a, (tuple, list))
    a_list = list(a) if split_a else [a]
    M = sum(x.shape[0] for x in a_list)
    K = a_list[0].shape[1]
    N = w.shape[-1]
    tk = K if tk is None else tk
    nk = K // tk
    assert M % tm == 0 and N % tn == 0 and K % tk == 0
    cast_w = nk == 1 and w.dtype != BF16
    has_bias = bias is not None
    has_res = res is not None
    has_norm = norm is not None
    modulate = has_norm and norm[1] is not None
    n_p = T_P // tm
    if split_a:
        assert M == T_ALL and a_list[0].shape[0] == T_P
        in_specs = [pl.BlockSpec((tm, tk), lambda j, i, k: (jnp.minimum(i, n_p - 1), k)),
                    pl.BlockSpec((tm, tk), lambda j, i, k: (jnp.maximum(i - n_p, 0), k))]
    else:
        in_specs = [pl.BlockSpec((tm, tk), lambda j, i, k: (i, k))]
    once = dict(pipeline_mode=pl.Buffered(1)) if (N == tn and nk == 1) else {}
    if layer is None:
        in_specs.append(pl.BlockSpec((tk, tn), lambda j, i, k: (k, j), **once))
    else:
        in_specs.append(pl.BlockSpec((None, tk, tn), lambda j, i, k: (layer, k, j), **once))
    args = a_list + [w]
    if has_bias:
        in_specs.append(pl.BlockSpec((1, tn), lambda j, i, k: (0, j)))
        args.append(bias.reshape(1, N))
    cond_spec = pl.BlockSpec((None, 1, tn), lambda j, i, k: (_cond_row(i, tm), 0, j))
    if has_res:
        in_specs += [pl.BlockSpec((tm, tn), lambda j, i, k: (i, j)), cond_spec]
        args += [res, gate]
    out_specs = pl.BlockSpec((tm, tn), lambda j, i, k: (i, j))
    out_shape = jax.ShapeDtypeStruct((M, N), out_dtype)
    h_bytes = 0
    if has_norm:
        assert tn == N
        g, shift, scale, h_dtype = norm
        in_specs.append(pl.BlockSpec((1, tn), lambda j, i, k: (0, 0)))
        args.append(g.reshape(1, N))
        if modulate:
            in_specs += [cond_spec, cond_spec]
            args += [shift, scale]
        out_specs = [out_specs, pl.BlockSpec((tm, tn), lambda j, i, k: (i, j))]
        out_shape = [out_shape, jax.ShapeDtypeStruct((M, N), h_dtype)]
        h_bytes = tm * tn * jnp.dtype(h_dtype).itemsize
    scratch = []
    if nk > 1:
        scratch.append(pltpu.VMEM((tm, tn), F32))
    if cast_w:
        scratch.append(pltpu.VMEM((tk, tn), BF16))
    w_bufs = 1 if once else 2
    vmem = (2 * (len(a_list) * tm * tk * a_list[0].dtype.itemsize + tm * tn * jnp.dtype(out_dtype).itemsize + h_bytes
                 + (tm * tn * 4 if has_res else 0)) + w_bufs * tk * tn * w.dtype.itemsize
            + 3 * tm * tn * 4 + (tk * tn * 2 if cast_w else 0) + tm * tk * 2 + (4 << 20))
    return pl.pallas_call(
        functools.partial(_mm_body, nk=nk, tm=tm, split_a=split_a, cast_w=cast_w, act=act, has_bias=has_bias,
                          has_res=has_res, has_norm=has_norm, modulate=modulate),
        grid=(N // tn, M // tm, nk),
        in_specs=in_specs,
        out_specs=out_specs,
        out_shape=out_shape,
        scratch_shapes=scratch,
        compiler_params=_params(3, vmem),
        name=name,
    )(*args)


def _ada_body(c_ref, w_ref, b_ref, o_ref):
    c = c_ref[...]
    s = (c * jax.nn.sigmoid(c)).astype(BF16)
    o_ref[...] = jnp.dot(s, w_ref[...].astype(BF16), preferred_element_type=F32) + b_ref[...]


def _ada_table(cond, ada_w, ada_b):
    tn = 1024
    n6 = 6 * D_MODEL
    return pl.pallas_call(
        _ada_body,
        grid=(DEPTH, n6 // tn),
        in_specs=[pl.BlockSpec((N_COND, D_MODEL), lambda l, j: (0, 0)),
                  pl.BlockSpec((None, D_MODEL, tn), lambda l, j: (l, 0, j)),
                  pl.BlockSpec((None, 1, tn), lambda l, j: (l, 0, j))],
        out_specs=pl.BlockSpec((None, N_COND, tn), lambda l, j: (l, 0, j)),
        out_shape=jax.ShapeDtypeStruct((DEPTH, N_COND, n6), F32),
        compiler_params=_params(2, 2 * D_MODEL * tn * 4 + D_MODEL * tn * 2 + (4 << 20)),
        name="ada_table",
    )(cond, ada_w, ada_b.reshape(DEPTH, 1, n6))


def _norm_body(x_ref, g_ref, sh_ref, sc_ref, o_ref, *, modulate):
    x = x_ref[...]
    y = x * lax.rsqrt(jnp.mean(x * x, axis=-1, keepdims=True) + NORM_EPS) * g_ref[...]
    if modulate:
        y = y * (1.0 + sc_ref[...]) + sh_ref[...]
    o_ref[...] = y.astype(o_ref.dtype)


def _norm_mod(x, g, shift, scale, out_dtype, modulate=True, row0=0, nrows=T_ALL):
    tm = 512
    t0 = row0 // tm
    cond_spec = pl.BlockSpec((None, 1, D_MODEL), lambda i: (_cond_row(t0 + i, tm), 0, 0))
    return pl.pallas_call(
        functools.partial(_norm_body, modulate=modulate),
        grid=(nrows // tm,),
        in_specs=[pl.BlockSpec((tm, D_MODEL), lambda i: (t0 + i, 0)),
                  pl.BlockSpec((1, D_MODEL), lambda i: (0, 0)),
                  cond_spec, cond_spec],
        out_specs=pl.BlockSpec((tm, D_MODEL), lambda i: (i, 0)),
        out_shape=jax.ShapeDtypeStruct((nrows, D_MODEL), out_dtype),
        compiler_params=_params(1, 6 * tm * D_MODEL * 4 + (4 << 20)),
        name="norm_mod",
    )(x, g.reshape(1, D_MODEL), shift, scale)


def _softmax_sink_pv(parts, sink_col):
    m = sink_col
    for s, _ in parts:
        m = jnp.maximum(m, jnp.max(s, axis=-1, keepdims=True))
    denom = jnp.exp(sink_col - m)
    out = None
    for s, v in parts:
        p = jnp.exp(s - m)
        denom = denom + jnp.sum(p, axis=-1, keepdims=True)
        pv = jnp.dot(p.astype(BF16), v, preferred_element_type=F32)
        out = pv if out is None else out + pv
    return out / denom


def _attn_ctx_body(sink_ref, q_ref, k_ref, v_ref, o_ref):
    h = pl.program_id(1)
    scale = ATTN_HEAD_DIM ** -0.5
    k = k_ref[...].astype(BF16)
    v = v_ref[...].astype(BF16)
    for g in range(ATTN_GROUP):
        q = q_ref[:, g * ATTN_HEAD_DIM:(g + 1) * ATTN_HEAD_DIM].astype(BF16)
        s = _dot_nt(q, k) * scale
        sink = jnp.full((SEQ, 1), sink_ref[h * ATTN_GROUP + g], F32)
        o = _softmax_sink_pv([(s, v)], sink)
        o_ref[:, g * ATTN_HEAD_DIM:(g + 1) * ATTN_HEAD_DIM] = o.astype(o_ref.dtype)


def _attn_context(qkv, sink):
    gw = ATTN_GROUP * ATTN_HEAD_DIM
    kcol = ATTN_HEADS * ATTN_HEAD_DIM // ATTN_HEAD_DIM
    vcol = kcol + ATTN_KV_HEADS
    return pl.pallas_call(
        _attn_ctx_body,
        grid=(BATCH, ATTN_KV_HEADS),
        in_specs=[pl.BlockSpec(memory_space=pltpu.SMEM),
                  pl.BlockSpec((SEQ, gw), lambda b, h: (b, h)),
                  pl.BlockSpec((SEQ, ATTN_HEAD_DIM), lambda b, h: (b, kcol + h)),
                  pl.BlockSpec((SEQ, ATTN_HEAD_DIM), lambda b, h: (b, vcol + h))],
        out_specs=pl.BlockSpec((SEQ, gw), lambda b, h: (b, h)),
        out_shape=jax.ShapeDtypeStruct((T_P, ATTN_HEADS * ATTN_HEAD_DIM), BF16),
        compiler_params=_params(2, 16 << 20),
        name="attn_context",
    )(sink, qkv, qkv, qkv)


def _rope(x, cos, sin_lo, sin_hi):
    return (x * cos + pltpu.roll(x, ATTN_HEAD_DIM - 32, axis=1) * sin_lo
            + pltpu.roll(x, 32, axis=1) * sin_hi)


def _attn_lat_body(sink_ref, q_ref, k_ref, v_ref, ck_ref, cv_ref, cos_ref, slo_ref, shi_ref, o_ref, kr_ref):
    h = pl.program_id(1)
    scale = ATTN_HEAD_DIM ** -0.5
    blk = ATTN_BLOCK
    nb = DEC_SEQ // blk
    rows = ATTN_GROUP * blk
    kr_ref[...] = _rope(k_ref[...], cos_ref[...], slo_ref[...], shi_ref[...]).astype(BF16)
    ck = ck_ref[...].astype(BF16)
    cv = cv_ref[...].astype(BF16)
    row_head = lax.broadcasted_iota(jnp.int32, (rows, 1), 0) // blk
    sink = jnp.zeros((rows, 1), F32)
    for g in range(ATTN_GROUP):
        sink = jnp.where(row_head == g, sink_ref[h * ATTN_GROUP + g], sink)
    q_in_blk = lax.broadcasted_iota(jnp.int32, (rows, 3 * blk), 0) % blk
    k_in_band = lax.broadcasted_iota(jnp.int32, (rows, 3 * blk), 1)

    def block(j, carry):
        r0 = pl.multiple_of(j * blk, blk)
        k0 = pl.multiple_of(jnp.clip(j - 1, 0, nb - 3) * blk, blk)
        cos = cos_ref[pl.ds(r0, blk), :]
        slo = slo_ref[pl.ds(r0, blk), :]
        shi = shi_ref[pl.ds(r0, blk), :]
        qs = jnp.concatenate(
            [_rope(q_ref[pl.ds(r0, blk), g * ATTN_HEAD_DIM:(g + 1) * ATTN_HEAD_DIM], cos, slo, shi)
             for g in range(ATTN_GROUP)], axis=0).astype(BF16)
        kl = kr_ref[pl.ds(k0, 3 * blk), :]
        vl = v_ref[pl.ds(k0, 3 * blk), :].astype(BF16)
        rel = (k0 + k_in_band) - (r0 + q_in_blk)
        s_loc = jnp.where(jnp.abs(rel) <= WINDOW, _dot_nt(qs, kl) * scale, NEG_INF)
        s_ctx = _dot_nt(qs, ck) * scale
        o = _softmax_sink_pv([(s_loc, vl), (s_ctx, cv)], sink)
        for g in range(ATTN_GROUP):
            o_ref[pl.ds(r0, blk), g * ATTN_HEAD_DIM:(g + 1) * ATTN_HEAD_DIM] = (
                o[g * blk:(g + 1) * blk].astype(o_ref.dtype))
        return carry

    lax.fori_loop(0, nb, block, 0)


def _rope_tables():
    t = np.arange(DEC_SEQ)
    half = ATTN_HEAD_DIM // 2
    inv = ROPE_BASE ** (-np.arange(0, half, 2, dtype=np.float64) / half)
    ang_r = (t // GRID_W).astype(np.float64)[:, None] * inv
    ang_c = (t % GRID_W).astype(np.float64)[:, None] * inv
    ang = np.concatenate([ang_r, ang_r, ang_c, ang_c], axis=1)
    lane = np.arange(ATTN_HEAD_DIM)
    first = (lane % 64) < 32
    cos = np.cos(ang)
    sin = np.sin(ang)
    sin_lo = np.where(first, -sin, 0.0)
    sin_hi = np.where(first, 0.0, sin)
    return (jnp.asarray(cos, F32), jnp.asarray(sin_lo, F32), jnp.asarray(sin_hi, F32))


def _attn_latent(qkv, cache_k, cache_v, layer, sink):
    gw = ATTN_GROUP * ATTN_HEAD_DIM
    kcol = ATTN_HEADS
    vcol = kcol + ATTN_KV_HEADS
    rb0 = T_P // DEC_SEQ
    cos, slo, shi = _rope_tables()
    tab = pl.BlockSpec((DEC_SEQ, ATTN_HEAD_DIM), lambda b, h: (0, 0))
    cache_spec = pl.BlockSpec((None, None, PAST_LEN, ATTN_HEAD_DIM), lambda b, h: (b, layer, 0, h))
    return pl.pallas_call(
        _attn_lat_body,
        grid=(DEC_BATCH, ATTN_KV_HEADS),
        in_specs=[pl.BlockSpec(memory_space=pltpu.SMEM),
                  pl.BlockSpec((DEC_SEQ, gw), lambda b, h: (rb0 + b, h)),
                  pl.BlockSpec((DEC_SEQ, ATTN_HEAD_DIM), lambda b, h: (rb0 + b, kcol + h)),
                  pl.BlockSpec((DEC_SEQ, ATTN_HEAD_DIM), lambda b, h: (rb0 + b, vcol + h)),
                  cache_spec, cache_spec, tab, tab, tab],
        out_specs=pl.BlockSpec((DEC_SEQ, gw), lambda b, h: (b, h)),
        out_shape=jax.ShapeDtypeStruct((T_S, ATTN_HEADS * ATTN_HEAD_DIM), BF16),
        scratch_shapes=[pltpu.VMEM((DEC_SEQ, ATTN_HEAD_DIM), BF16)],
        compiler_params=_params(2, 32 << 20),
        name="attn_latent",
    )(sink, qkv, qkv, qkv, cache_k, cache_v, cos, slo, shi)


def _dft_matrices(L):
    N = 2 * L
    p = np.arange(N)[:, None]
    n = np.arange(N)[None, :]
    f = np.where(p < L, p, p - L)
    ang = 2.0 * np.pi * ((f * n) % N) / N
    fwd = np.where(p < L, np.cos(ang), np.where(p == L, np.cos(np.pi * n), -np.sin(ang)))
    t = np.arange(L)[:, None]
    pp = np.arange(N)[None, :]
    ff = np.where(pp < L, pp, pp - L)
    ang2 = 2.0 * np.pi * ((ff * t) % N) / N
    inv = np.where(pp < L, np.where(pp == 0, 1.0, 2.0) * np.cos(ang2),
                   np.where(pp == L, np.cos(np.pi * t), -2.0 * np.sin(ang2))) / N
    return fwd, inv


def _hyena_filter_time(L, f_w1, f_b1, f_w2, f_b2, f_w3, f_b3, freq, f_wout):
    t = jnp.linspace(0.0, 1.0, L, dtype=F32)[:, None]
    bands = (HYENA_EMB - 1) // 2
    w = 2 * math.pi * jnp.arange(L, dtype=F32)[:, None] / L
    fr = jnp.linspace(1e-4, bands - 1, bands, dtype=F32)[None, :]
    z = jnp.concatenate([t, jnp.cos(fr * w), -jnp.sin(fr * w)], axis=-1)
    hdn = jnp.sin(freq * (z @ f_w1 + f_b1))
    hdn = jnp.sin(freq * (hdn @ f_w2 + f_b2))
    hdn = jnp.sin(freq * (hdn @ f_w3 + f_b3))
    filt = (hdn @ f_wout).reshape(L, 2, D_MODEL)
    deltas = jnp.linspace(math.log(HYENA_TARGET) / HYENA_SLOW_DECAY,
                          math.log(HYENA_TARGET) / HYENA_FAST_DECAY, D_MODEL, dtype=F32)
    filt = filt * jnp.exp(-t * jnp.abs(deltas))[:, None, :]
    return filt.reshape(2 * L, D_MODEL)


def _hyena_body(x0_ref, x1_ref, v_ref, cw0_ref, cw1_ref, cwv_ref, cb0_ref, cb1_ref, cbv_ref,
                hf_ref, fb_ref, fwd_ref, inv_ref, o_ref, *, L, tc, width):
    row = lax.broadcasted_iota(jnp.int32, (L, 1), 0)
    first = row == 0

    def chain(cols):
        def conv3(u_ref, w_ref, b_ref):
            u = u_ref[:, cols]
            prev = jnp.where(first, 0.0, pltpu.roll(u, 1, axis=0))
            nxt = jnp.where(row == L - 1, 0.0, pltpu.roll(u, L - 1, axis=0))
            return prev * w_ref[0:1, cols] + u * w_ref[1:2, cols] + nxt * w_ref[2:3, cols] + b_ref[:, cols]

        x0 = conv3(x0_ref, cw0_ref, cb0_ref)
        x1 = conv3(x1_ref, cw1_ref, cb1_ref)
        v = conv3(v_ref, cwv_ref, cbv_ref)
        z = v * x1
        yield
        zf = jnp.dot(fwd_ref[...], z.astype(BF16), preferred_element_type=F32)
        yield
        zt, zb = zf[:L], zf[L:]
        ht, hb = hf_ref[0:L, cols], hf_ref[L:2 * L, cols]
        yt = zt * ht - jnp.where(first, 0.0, zb * hb)
        yb = jnp.where(first, zb * hb, zt * hb + zb * ht)
        yf = jnp.concatenate([yt, yb], axis=0).astype(BF16)
        yield
        y = jnp.dot(inv_ref[...], yf, preferred_element_type=F32)
        yield
        o_ref[:, cols] = ((y + z * fb_ref[:, cols]) * x0).astype(o_ref.dtype)

    _interleave([chain(slice(c, c + width)) for c in range(0, tc, width)], stagger=1)


def _hyena_mix(u, row0, nseq, L, conv_w, conv_b, hf, fbias):
    tc, width = 512, 256
    nc = D_MODEL // tc
    fwd, inv = _dft_matrices(L)
    fwd = jnp.asarray(fwd[:, :L], F32).astype(BF16)
    inv = jnp.asarray(inv, F32).astype(BF16)
    rb0 = row0 // L

    def uspec(part):
        return pl.BlockSpec((L, tc), lambda b, j: (rb0 + b, part * nc + j))

    def wspec(part):
        return pl.BlockSpec((3, tc), lambda b, j: (0, part * nc + j))

    def bspec(part):
        return pl.BlockSpec((1, tc), lambda b, j: (0, part * nc + j))

    cb = conv_b.reshape(1, 3 * D_MODEL)
    in_specs = [uspec(0), uspec(1), uspec(2), wspec(0), wspec(1), wspec(2), bspec(0), bspec(1), bspec(2),
                pl.BlockSpec((2 * L, tc), lambda b, j: (0, j)),
                pl.BlockSpec((1, tc), lambda b, j: (0, j)),
                pl.BlockSpec((2 * L, L), lambda b, j: (0, 0), pipeline_mode=pl.Buffered(1)),
                pl.BlockSpec((L, 2 * L), lambda b, j: (0, 0), pipeline_mode=pl.Buffered(1))]
    args = [u, u, u, conv_w, conv_w, conv_w, cb, cb, cb, hf, fbias.reshape(1, D_MODEL), fwd, inv]
    vmem = 2 * (3 * L * tc * 4 + 2 * L * tc * 4 + L * tc * 2) + 2 * (2 * L * L * 2) + 12 * L * tc * 4 + (4 << 20)
    return pl.pallas_call(
        functools.partial(_hyena_body, L=L, tc=tc, width=width),
        grid=(nseq, nc),
        in_specs=in_specs,
        out_specs=pl.BlockSpec((L, tc), lambda b, j: (b, j)),
        out_shape=jax.ShapeDtypeStruct((nseq * L, D_MODEL), BF16),
        compiler_params=_params(2, vmem),
        name="hyena_mix",
    )(*args)


def _hyena_spectrum(L, filt_params):
    filt = _hyena_filter_time(L, *filt_params)
    fwd, _ = _dft_matrices(L)
    n = 2 * L
    t = np.arange(L)
    comb = np.zeros((n, n))
    comb[:, 2 * t] = fwd[:, t]
    comb[:, 2 * t[1:] + 1] = fwd[:, n - t[1:]]
    return _matmul(jnp.asarray(comb, F32), filt, tm=min(n, 1024), tn=512, out_dtype=F32, name="hyena_spectrum")


def _rwkv_mix_body(x_ref, xp_ref, xn_ref, g_ref, sh_ref, sc_ref, mu_ref, *o_refs, tm):
    i = pl.program_id(0)
    n_p = T_P // tm
    per = DEC_SEQ // tm
    pos = (i - n_p) % per
    first = jnp.logical_or(i < n_p, pos == 0)
    last = jnp.logical_or(i < n_p, pos == per - 1)

    def norm(x):
        y = x * lax.rsqrt(jnp.mean(x * x, axis=-1, keepdims=True) + NORM_EPS) * g_ref[...]
        return y * (1.0 + sc_ref[...]) + sh_ref[...]

    h = norm(x_ref[...])
    row = lax.broadcasted_iota(jnp.int32, (tm, 1), 0)
    before = jnp.where(first, 0.0, norm(xp_ref[...])[7:8, :])
    after = jnp.where(last, 0.0, norm(xn_ref[...])[0:1, :])
    prev = jnp.where(row == 0, before, pltpu.roll(h, 1, axis=0))
    nxt = jnp.where(row == tm - 1, after, pltpu.roll(h, tm - 1, axis=0))
    xx = 0.5 * (prev + nxt) - h
    for n, o_ref in enumerate(o_refs):
        o_ref[...] = (h + xx * mu_ref[n:n + 1, :]).astype(o_ref.dtype)


def _rwkv_token_mix(x, g, shift, scale, mu):
    tm = SEQ
    hb = tm // 8
    nrb = T_ALL // 8
    out = jax.ShapeDtypeStruct((T_ALL, D_MODEL), BF16)
    cond_spec = pl.BlockSpec((None, 1, D_MODEL), lambda i: (_cond_row(i, tm), 0, 0))
    return pl.pallas_call(
        functools.partial(_rwkv_mix_body, tm=tm),
        grid=(T_ALL // tm,),
        in_specs=[pl.BlockSpec((tm, D_MODEL), lambda i: (i, 0)),
                  pl.BlockSpec((8, D_MODEL), lambda i: (jnp.maximum(i * hb - 1, 0), 0)),
                  pl.BlockSpec((8, D_MODEL), lambda i: (jnp.minimum((i + 1) * hb, nrb - 1), 0)),
                  pl.BlockSpec((1, D_MODEL), lambda i: (0, 0)),
                  cond_spec, cond_spec,
                  pl.BlockSpec((6, D_MODEL), lambda i: (0, 0))],
        out_specs=[pl.BlockSpec((tm, D_MODEL), lambda i: (i, 0))] * 6,
        out_shape=[out] * 6,
        compiler_params=_params(1, 32 << 20),
        name="rwkv_token_mix",
    )(x, x, x, g.reshape(1, D_MODEL), shift, scale, mu)


def _bdot(a, b):
    return jnp.dot(a.astype(BF16), b.astype(BF16), preferred_element_type=F32)


TRI_BASE = 8


def _tri_inverse_steps(n):
    m = n.shape[0]
    ri = lax.broadcasted_iota(jnp.int32, n.shape, 0)
    rj = lax.broadcasted_iota(jnp.int32, n.shape, 1)
    eye = (ri == rj).astype(F32)
    nb = jnp.where(ri // TRI_BASE == rj // TRI_BASE, n, 0.0)
    p = eye - nb
    q = _bdot(nb, nb)
    yield
    qp = _bdot(jnp.concatenate([q, p], axis=0), q)
    q, p = qp[:m], p + qp[m:]
    yield
    t = p + _bdot(p, q)
    yield
    s = TRI_BASE
    while s < RWKV_CHUNK:
        off = jnp.where((ri // (2 * s) == rj // (2 * s)) & (ri // s != rj // s), n, 0.0)
        mt = _bdot(off, t)
        yield
        t = t - _bdot(t, mt)
        yield
        s *= 2
    return t


def _interleave(gens, stagger=0):
    results = [None] * len(gens)
    live = list(enumerate(gens))
    rounds = 0
    while live:
        still = []
        for n, g in live:
            if rounds < stagger * n:
                still.append((n, g))
                continue
            try:
                next(g)
                still.append((n, g))
            except StopIteration as done:
                results[n] = done.value
        live = still
        rounds += 1
    return results


def _rwkv_scan_body(*refs, L, has_init, pairs):
    it = iter(refs)
    r_ref, k_ref, v_ref, wlf_ref, wlb_ref, alf_ref, alb_ref, kk_ref, ka_ref = (next(it) for _ in range(9))
    s0_ref = next(it) if has_init else None
    yf_ref, yb_ref, sfin_ref, s_ref = next(it), next(it), next(it), next(it)
    C = RWKV_CHUNK
    HD = RWKV_HEAD_DIM
    nc = L // C
    lane = lax.broadcasted_iota(jnp.int32, (1, LANE), 1)
    head0 = lane < HD
    ones_bd = ((lax.broadcasted_iota(jnp.int32, (LANE, LANE), 0) // HD)
               == (lax.broadcasted_iota(jnp.int32, (LANE, LANE), 1) // HD)).astype(BF16)
    ci = lax.broadcasted_iota(jnp.int32, (C, C), 0)
    cj = lax.broadcasted_iota(jnp.int32, (C, C), 1)
    bi = lax.broadcasted_iota(jnp.int32, (2 * C, 2 * C), 0) % C
    bj = lax.broadcasted_iota(jnp.int32, (2 * C, 2 * C), 1) % C

    def stack(x):
        return jnp.concatenate([jnp.where(head0, x, 0.0), jnp.where(head0, 0.0, x)], axis=0)

    def chunk(r, k, v, wl, al, kk_w, ka_w, s, reverse):
        sp = jnp.maximum(-wl, 0.0) + jnp.log(1.0 + jnp.exp(-jnp.abs(wl)))
        lw = -jnp.exp(-sp - 0.5)
        a = jax.nn.sigmoid(al)
        kkr = k * kk_w
        kk = kkr * lax.rsqrt(_bdot(kkr * kkr, ones_bd) + 1e-12)
        kd = k * (1.0 + (a - 1.0) * ka_w)
        b = kk * a
        tri = ((cj >= ci) if reverse else (cj <= ci)).astype(BF16)
        lw_hi = lw.astype(BF16)
        lw_lo = (lw - lw_hi.astype(F32)).astype(BF16)
        g_inc = (jnp.dot(tri, lw_hi, preferred_element_type=F32)
                 + jnp.dot(tri, lw_lo, preferred_element_type=F32))
        yield
        g_end = g_inc[0:1, :] if reverse else g_inc[C - 1:C, :]
        e_neg = jnp.exp(-g_inc)
        e_end = jnp.exp(g_end - g_inc)
        lk = stack(kk * jnp.exp(g_inc - lw)).astype(BF16)
        lr = stack(r * jnp.exp(g_inc)).astype(BF16)
        bs = stack(b * e_neg).astype(BF16)
        ks = stack(kd * e_neg).astype(BF16)
        vs = stack(v)
        vs_b = vs.astype(BF16)
        big = _dot_nt(jnp.concatenate([lk, lr], axis=0), jnp.concatenate([bs, ks], axis=0))
        yield
        strict = (bj > bi) if reverse else (bj < bi)
        incl = (bj >= bi) if reverse else (bj <= bi)
        n_bd = jnp.where(strict, big[:2 * C, :2 * C], 0.0)
        m_dk = jnp.where(strict, big[:2 * C, 2 * C:], 0.0)
        a_all = jnp.where(jnp.concatenate([incl, incl], axis=1), big[2 * C:, :], 0.0)
        s_b = s.astype(BF16)
        w = _dot_nt(lk, s_b) + _bdot(m_dk, vs_b)
        ys0 = _dot_nt(lr, s_b)
        t_inv = yield from _tri_inverse_steps(n_bd)
        us = -_bdot(t_inv, w)
        yield
        uv = jnp.concatenate([us, vs], axis=0)
        ys = ys0 + _bdot(a_all, uv)
        s_new = s * jnp.exp(g_end) + _bdot(uv.T, jnp.concatenate([stack(b * e_end), stack(kd * e_end)], axis=0))
        return ys[:C] + ys[C:], s_new

    z = jnp.zeros((HD, HD), F32)
    for d in range(2):
        for p in range(pairs):
            if has_init:
                s_ref[d, p] = jnp.concatenate([jnp.concatenate([s0_ref[d, 2 * p], z], axis=1),
                                               jnp.concatenate([z, s0_ref[d, 2 * p + 1]], axis=1)], axis=0)
            else:
                s_ref[d, p] = jnp.zeros((LANE, LANE), F32)

    def step(c, carry):
        rf = pl.ds(pl.multiple_of(c * C, C), C)
        rb = pl.ds(pl.multiple_of((nc - 1 - c) * C, C), C)
        jobs = []
        for p in range(pairs):
            cols = slice(p * LANE, (p + 1) * LANE)
            for d, (rows, wl_ref, al_ref) in enumerate(((rf, wlf_ref, alf_ref), (rb, wlb_ref, alb_ref))):
                jobs.append((r_ref[rows, cols], k_ref[rows, cols], v_ref[rows, cols], wl_ref[rows, cols],
                             al_ref[rows, cols], kk_ref[:, cols], ka_ref[:, cols], s_ref[d, p], d == 1))
        results = _interleave([chunk(*job) for job in jobs])
        for n, (y, s_new) in enumerate(results):
            p, d = divmod(n, 2)
            cols = slice(p * LANE, (p + 1) * LANE)
            if d == 0:
                yf_ref[rf, cols] = y
            else:
                yb_ref[rb, cols] = y
            s_ref[d, p] = s_new
        return carry

    lax.fori_loop(0, nc, step, 0)
    for d in range(2):
        for p in range(pairs):
            s = s_ref[d, p]
            sfin_ref[d, 2 * p] = s[:HD, :HD]
            sfin_ref[d, 2 * p + 1] = s[HD:, HD:]


def _rwkv_scan(r, k, v, wl_f, wl_b, al_f, al_b, k_k, k_a, s0, row0, nseq, L):
    pairs = 4
    width = pairs * LANE
    rb0 = row0 // L
    blk = pl.BlockSpec((L, width), lambda b, hp: (rb0 + b, hp))
    vec = pl.BlockSpec((1, width), lambda b, hp: (0, hp))
    st = pl.BlockSpec((None, 2, 2 * pairs, RWKV_HEAD_DIM, RWKV_HEAD_DIM), lambda b, hp: (b, 0, hp, 0, 0))
    has_init = s0 is not None
    in_specs = [blk] * 7 + [vec, vec] + ([st] if has_init else [])
    args = [r, k, v, wl_f, wl_b, al_f, al_b, k_k.reshape(1, D_MODEL), k_a.reshape(1, D_MODEL)]
    if has_init:
        args.append(s0)
    yshape = jax.ShapeDtypeStruct((nseq * L, D_MODEL), F32)
    return pl.pallas_call(
        functools.partial(_rwkv_scan_body, L=L, has_init=has_init, pairs=pairs),
        grid=(nseq, RWKV_HEADS // (2 * pairs)),
        in_specs=in_specs,
        out_specs=[pl.BlockSpec((L, width), lambda b, hp: (b, hp))] * 2 + [st],
        out_shape=[yshape, yshape,
                   jax.ShapeDtypeStruct((nseq, 2, RWKV_HEADS, RWKV_HEAD_DIM, RWKV_HEAD_DIM), F32)],
        scratch_shapes=[pltpu.VMEM((2, pairs, LANE, LANE), F32)],
        compiler_params=_params(2, 2 * 9 * L * width * 4 + (8 << 20)),
        name="rwkv_scan",
    )(*args)


def _rwkv_post_body(yfp_ref, ybp_ref, yfs_ref, ybs_ref, r_ref, k_ref, v_ref, alf_ref, alb_ref, g_ref,
                    ka_ref, rk_ref, lnw_ref, lnb_ref, o_ref, *, tm, tc):
    HD = RWKV_HEAD_DIM
    is_prompt = pl.program_id(0) < T_P // tm
    same_head = ((lax.broadcasted_iota(jnp.int32, (LANE, LANE), 0) // HD)
                 == (lax.broadcasted_iota(jnp.int32, (LANE, LANE), 1) // HD))
    ones_bd = same_head.astype(BF16)

    def head_sum(x):
        hi = x.astype(BF16)
        lo = (x - hi.astype(F32)).astype(BF16)
        return (jnp.dot(hi, ones_bd, preferred_element_type=F32) + jnp.dot(lo, ones_bd, preferred_element_type=F32))

    for c in range(tc // LANE):
        sl = slice(c * LANE, (c + 1) * LANE)
        y = jnp.where(is_prompt, yfp_ref[:, sl] + ybp_ref[:, sl], yfs_ref[:, sl] + ybs_ref[:, sl])
        mean = head_sum(y) * (1.0 / HD)
        yc = y - mean
        var = head_sum(yc * yc) * (1.0 / HD)
        yn = yc * lax.rsqrt(var + RWKV_GN_EPS) * lnw_ref[:, sl] + lnb_ref[:, sl]
        k = k_ref[:, sl]
        ka = ka_ref[:, sl]
        kd_sum = (k * (1.0 + (jax.nn.sigmoid(alf_ref[:, sl]) - 1.0) * ka)
                  + k * (1.0 + (jax.nn.sigmoid(alb_ref[:, sl]) - 1.0) * ka))
        bonus = head_sum(r_ref[:, sl] * kd_sum * rk_ref[:, sl]) * v_ref[:, sl]
        o_ref[:, sl] = ((yn + bonus) * g_ref[:, sl]).astype(o_ref.dtype)


def _rwkv_post(y_prompt, y_latent, r, k, v, al_f, al_b, g, k_a, r_k, ln_w, ln_b):
    tm, tc = 512, 512
    n_p = T_P // tm
    blk = pl.BlockSpec((tm, tc), lambda i, j: (i, j))
    blk_p = pl.BlockSpec((tm, tc), lambda i, j: (jnp.minimum(i, n_p - 1), j))
    blk_s = pl.BlockSpec((tm, tc), lambda i, j: (jnp.maximum(i - n_p, 0), j))
    vec = pl.BlockSpec((1, tc), lambda i, j: (0, j))
    row = lambda x: x.reshape(1, D_MODEL)
    return pl.pallas_call(
        functools.partial(_rwkv_post_body, tm=tm, tc=tc),
        grid=(T_ALL // tm, D_MODEL // tc),
        in_specs=[blk_p, blk_p, blk_s, blk_s] + [blk] * 6 + [vec] * 4,
        out_specs=blk,
        out_shape=jax.ShapeDtypeStruct((T_ALL, D_MODEL), BF16),
        compiler_params=_params(2, 40 << 20),
        name="rwkv_post",
    )(*y_prompt, *y_latent, r, k, v, al_f, al_b, g, row(k_a), row(r_k), row(ln_w), row(ln_b))


def _pad_lora(w_down, w_up):
    inner = w_down.shape[-1]
    pad = (-inner) % LANE
    return (jnp.pad(w_down, ((0, 0), (0, pad))), jnp.pad(w_up, ((0, pad), (0, 0))))


def kernel(x_prompt, x_sample, cache_attn_k, cache_attn_v, state_rwkv, c, c_ctx, ada_w, ada_b, norm1_g, norm2_g, mlp_w1, mlp_w2, attn_wqkv, attn_wo, attn_sink, hy_w_in, hy_conv_w, hy_conv_b, hy_f_w1, hy_f_b1, hy_f_w2, hy_f_b2, hy_f_w3, hy_f_b3, hy_f_freq, hy_f_wout, hy_bias, hy_w_out, rw_mu, rw_wr, rw_wk, rw_wv, rw_wo, rw_w0, rw_w1, rw_w2, rw_a0, rw_a1, rw_a2, rw_g1, rw_g2, rw_k_k, rw_k_a, rw_r_k, rw_ln_w, rw_ln_b, final_norm_g):
    D = D_MODEL
    x = jnp.concatenate([x_prompt.reshape(T_P, D), x_sample.reshape(T_S, D)], axis=0)
    cond = jnp.concatenate([c_ctx[None, :], c, jnp.zeros((N_COND - 1 - DEC_BATCH, D), F32)], axis=0)
    mod = _ada_table(cond, ada_w, ada_b).reshape(DEPTH, N_COND, 6, 1, D)
    kv_w = ATTN_KV_HEADS * ATTN_HEAD_DIM
    ck_all = cache_attn_k.reshape(DEC_BATCH, -1, PAST_LEN, kv_w)
    cv_all = cache_attn_v.reshape(DEC_BATCH, -1, PAST_LEN, kv_w)
    wide = dict(tm=1024, tn=1024)
    square = dict(tm=512, tn=D)
    bf = lambda w: w.astype(BF16)
    attn_wo_b, hy_w_out_b, mlp_w2_b = bf(attn_wo), bf(hy_w_out), bf(mlp_w2)
    rw_wr_b, rw_wk_b, rw_wv_b, rw_wo_b, rw_g2_b = bf(rw_wr), bf(rw_wk), bf(rw_wv), bf(rw_wo), bf(rw_g2)
    new_k, new_v, new_s = [], [], []
    for i in range(DEPTH):
        kind, j = i % N_MIXERS, i // N_MIXERS
        sh1, sc1, gt1, sh2, sc2, gt2 = (mod[i, :, n] for n in range(6))
        mix_out = dict(res=x, gate=gt1, norm=(norm2_g[i], sh2, sc2, BF16), **square)
        if kind == 0:
            h = _norm_mod(x, norm1_g[i], sh1, sc1, BF16)
            qkv = _matmul(h, attn_wqkv, layer=j, out_dtype=F32, **wide)
            qd = ATTN_HEADS * ATTN_HEAD_DIM
            new_k.append(qkv[:T_P, qd:qd + kv_w].reshape(BATCH, SEQ, ATTN_KV_HEADS, ATTN_HEAD_DIM))
            new_v.append(qkv[:T_P, qd + kv_w:].reshape(BATCH, SEQ, ATTN_KV_HEADS, ATTN_HEAD_DIM))
            o = (_attn_context(qkv, attn_sink[j]), _attn_latent(qkv, ck_all, cv_all, j, attn_sink[j]))
            x, h = _matmul(o, attn_wo_b, layer=j, **mix_out)
        elif kind == 1:
            h = _norm_mod(x, norm1_g[i], sh1, sc1, BF16)
            u = _matmul(h, hy_w_in, layer=j, out_dtype=F32, **wide)
            filt = (hy_f_w1[j], hy_f_b1[j], hy_f_w2[j], hy_f_b2[j], hy_f_w3[j], hy_f_b3[j],
                    hy_f_freq[j], hy_f_wout[j])
            o = (_hyena_mix(u, 0, BATCH, SEQ, hy_conv_w[j], hy_conv_b[j], _hyena_spectrum(SEQ, filt), hy_bias[j]),
                 _hyena_mix(u, T_P, DEC_BATCH, DEC_SEQ, hy_conv_w[j], hy_conv_b[j],
                            _hyena_spectrum(DEC_SEQ, filt), hy_bias[j]))
            x, h = _matmul(o, hy_w_out_b, layer=j, **mix_out)
        else:
            xr, xw, xk, xv, xa, xg = _rwkv_token_mix(x, norm1_g[i], sh1, sc1, rw_mu[j])
            r = _matmul(xr, rw_wr_b, layer=j, **square)
            k = _matmul(xk, rw_wk_b, layer=j, **square)
            v = _matmul(xv, rw_wv_b, layer=j, **square)
            g = _matmul(_matmul(xg, rw_g1, layer=j, tm=1024, tn=rw_g1.shape[-1], out_dtype=BF16, act="sigmoid"),
                        rw_g2_b, layer=j, **square)
            wl, al = [], []
            for d in range(2):
                w1p, w2p = _pad_lora(rw_w1[j, d], rw_w2[j, d])
                a1p, a2p = _pad_lora(rw_a1[j, d], rw_a2[j, d])
                wl.append(_matmul(_matmul(xw, w1p, tm=1024, tn=LANE, out_dtype=BF16, act="tanh"), w2p,
                                  bias=rw_w0[j, d], **square))
                al.append(_matmul(_matmul(xa, a1p, tm=1024, tn=LANE, out_dtype=BF16), a2p,
                                  bias=rw_a0[j, d], **square))
            scan = (r, k, v, wl[0], wl[1], al[0], al[1], rw_k_k[j], rw_k_a[j])
            yf_p, yb_p, s_fin = _rwkv_scan(*scan, None, 0, BATCH, SEQ)
            yf_s, yb_s, _ = _rwkv_scan(*scan, state_rwkv[:, j], T_P, DEC_BATCH, DEC_SEQ)
            new_s.append(s_fin)
            o = _rwkv_post((yf_p, yb_p), (yf_s, yb_s), r, k, v, al[0], al[1], g,
                           rw_k_a[j], rw_r_k[j], rw_ln_w[j], rw_ln_b[j])
            x, h = _matmul(o, rw_wo_b, layer=j, **mix_out)
        hid = _matmul(h, mlp_w1, layer=i, out_dtype=BF16, act="relu2", name="mlp_up", **wide)
        x = _matmul(hid, mlp_w2_b, layer=i, tm=512, tn=512, res=x, gate=gt2, name="mlp_down")
    final = lambda row0, nrows: _norm_mod(x, final_norm_g, mod[0, :, 0], mod[0, :, 0], F32, modulate=False,
                                          row0=row0, nrows=nrows)
    return (final(0, T_P).reshape(BATCH, SEQ, D), final(T_P, T_S).reshape(DEC_BATCH, DEC_SEQ, D),
            jnp.stack(new_k, axis=1), jnp.stack(new_v, axis=1), jnp.stack(new_s, axis=1))
```
